```python
import math
import jax, jax.numpy as jnp
from jax import lax
import numpy as np

D_MODEL = 1024
BATCH = 8
SEQ = 4096
DEPTH = 2

MIX_WIDTH = D_MODEL
SGU_WIDTH = MIX_WIDTH // 2
SGU_HEADS = 4
SGU_HEAD_DIM = SGU_WIDTH // SGU_HEADS
CHUNK = 128
ATT_WIDTH = MIX_WIDTH - SGU_WIDTH
ATT_HEADS = 8
ATT_HEAD_DIM = ATT_WIDTH // ATT_HEADS
DILATED_PATTERNS = ((128, 1), (512, 4), (2048, 16))
ATT_BLOCK = 128
ROPE_THETA = 10000.0
D_FF = 2816
IN_WIDTH = 2 * SGU_WIDTH + 3 * ATT_WIDTH
N_ADA = 9
EPS = 1e-6

kernel_name = "hybrid_sgu_dilated_macaron_adaln"


def rmsnorm(x, g):
    xf = x.astype(jnp.float32)
    y = xf * lax.rsqrt(jnp.mean(xf * xf, axis=-1, keepdims=True) + EPS)
    return (y * g.astype(jnp.float32)).astype(x.dtype)


def modulate(h, shift, scale):
    return h * (1.0 + scale[:, None, :]) + shift[:, None, :]


def swiglu(y, w_gate, w_up, w_down):
    return (jax.nn.silu(y @ w_gate) * (y @ w_up)) @ w_down


def rope_tables(S, dh, dtype):
    inv = ROPE_THETA ** (-jnp.arange(0, dh, 2, dtype=jnp.float32) / dh)
    ang = jnp.arange(S, dtype=jnp.float32)[:, None] * inv[None, :]
    ang = jnp.concatenate([ang, ang], axis=-1)
    return jnp.cos(ang)[:, None, :].astype(dtype), jnp.sin(ang)[:, None, :].astype(dtype)


def apply_rope(t, cos, sin):
    half = t.shape[-1] // 2
    rot = jnp.concatenate([-t[..., half:], t[..., :half]], axis=-1)
    return t * cos + rot * sin


def spatial_gating(u, v, ln_g, ln_b, w_s, b_s):
    B, S, H, dh = u.shape
    u = jax.nn.gelu(u)
    v = jax.nn.gelu(v)
    vf = v.astype(jnp.float32)
    mu = jnp.mean(vf, axis=-1, keepdims=True)
    var = jnp.mean(jnp.square(vf - mu), axis=-1, keepdims=True)
    vn = ((vf - mu) * lax.rsqrt(var + EPS)).astype(v.dtype) * ln_g + ln_b
    vc = vn.reshape(B, S // CHUNK, CHUNK, H, dh)
    causal = jnp.tril(jnp.ones((CHUNK, CHUNK), dtype=bool))
    ws = jnp.where(causal[None], w_s, jnp.zeros_like(w_s))
    z = jnp.einsum('hij,bnjhc->bnihc', ws, vc) + b_s.T[None, None, :, :, None]
    return u * z.reshape(B, S, H, dh)


def dilated_branch(q, k, v, window, dil):
    B, S, H, dh = q.shape
    n_back = window // dil
    span = dil * ATT_BLOCK
    S_pad = -(-S // span) * span
    L = S_pad // dil
    nb = L // ATT_BLOCK

    def to_blocks(t):
        t = jnp.pad(t, ((0, 0), (0, S_pad - S), (0, 0), (0, 0)))
        t = t.reshape(B, L, dil, H, dh).transpose(0, 2, 3, 1, 4)
        return t.reshape(B, dil, H, nb, ATT_BLOCK, dh)

    def with_prev(t):
        prev = jnp.concatenate([jnp.zeros_like(t[:, :, :, :1]), t[:, :, :, :-1]], axis=3)
        return jnp.concatenate([prev, t], axis=4)

    qb = to_blocks(q)
    kk = with_prev(to_blocks(k))
    vv = with_prev(to_blocks(v))
    s = jnp.einsum('brhnqd,brhnkd->brhnqk', qb, kk,
                   preferred_element_type=jnp.float32) * (1.0 / math.sqrt(dh))
    qi = jnp.arange(ATT_BLOCK)[:, None]
    ki = jnp.arange(2 * ATT_BLOCK)[None, :]
    diff = qi + ATT_BLOCK - ki
    band = (diff >= 0) & (diff <= n_back)
    valid = (jnp.arange(nb)[:, None, None] > 0) | (ki[None] >= ATT_BLOCK)
    mask = band[None] & valid
    s = jnp.where(mask, s, -jnp.inf)
    m = jnp.max(s, axis=-1, keepdims=True)
    p = jnp.exp(s - m)
    den = jnp.sum(p, axis=-1, keepdims=True)
    o = jnp.einsum('brhnqk,brhnkd->brhnqd', p, vv.astype(jnp.float32)) / den
    lse = (m + jnp.log(den))[..., 0]
    o = o.reshape(B, dil, H, L, dh).transpose(0, 3, 1, 2, 4).reshape(B, S_pad, H, dh)[:, :S]
    lse = lse.reshape(B, dil, H, L).transpose(0, 3, 1, 2).reshape(B, S_pad, H)[:, :S]
    return o, lse


def dilated_mixture(q, k, v):
    outs, lses = [], []
    for window, dil in DILATED_PATTERNS:
        o, lse = dilated_branch(q, k, v, window, dil)
        outs.append(o)
        lses.append(lse)
    w = jax.nn.softmax(jnp.stack(lses, axis=0), axis=0)
    o = jnp.sum(w[..., None] * jnp.stack(outs, axis=0), axis=0)
    return o.astype(q.dtype)


def setup_inputs(seed: int = 0) -> dict:
    key = jax.random.key(seed)
    ks = jax.random.split(key, 20)
    f32 = jnp.float32
    nrm = lambda k, shape, scale: jax.random.normal(k, shape, f32) * scale
    return {
        "x": nrm(ks[0], (BATCH, SEQ, D_MODEL), 1.0),
        "c": nrm(ks[1], (BATCH, D_MODEL), 1.0),
        "ada_w": nrm(ks[2], (DEPTH, D_MODEL, N_ADA * D_MODEL), D_MODEL ** -0.5),
        "ada_b": nrm(ks[3], (DEPTH, N_ADA * D_MODEL), 0.02),
        "norm_g": 1.0 + nrm(ks[4], (DEPTH, 3, D_MODEL), 0.02),
        "ffn1_wg": nrm(ks[5], (DEPTH, D_MODEL, D_FF), D_MODEL ** -0.5),
        "ffn1_wu": nrm(ks[6], (DEPTH, D_MODEL, D_FF), D_MODEL ** -0.5),
        "ffn1_wd": nrm(ks[7], (DEPTH, D_FF, D_MODEL), D_FF ** -0.5),
        "ffn2_wg": nrm(ks[8], (DEPTH, D_MODEL, D_FF), D_MODEL ** -0.5),
        "ffn2_wu": nrm(ks[9], (DEPTH, D_MODEL, D_FF), D_MODEL ** -0.5),
        "ffn2_wd": nrm(ks[10], (DEPTH, D_FF, D_MODEL), D_FF ** -0.5),
        "w_in": nrm(ks[11], (DEPTH, D_MODEL, IN_WIDTH), D_MODEL ** -0.5),
        "sgu_ln_g": 1.0 + nrm(ks[12], (DEPTH, SGU_HEADS, SGU_HEAD_DIM), 0.02),
        "sgu_ln_b": nrm(ks[13], (DEPTH, SGU_HEADS, SGU_HEAD_DIM), 0.02),
        "sgu_w": nrm(ks[14], (DEPTH, SGU_HEADS, CHUNK, CHUNK), CHUNK ** -0.5),
        "sgu_b": 1.0 + nrm(ks[15], (DEPTH, SGU_HEADS, CHUNK), 0.02),
        "w_out": nrm(ks[16], (DEPTH, MIX_WIDTH, D_MODEL), MIX_WIDTH ** -0.5),
        "final_g": 1.0 + nrm(ks[17], (D_MODEL,), 0.02),
    }


def reference(x, c, ada_w, ada_b, norm_g, ffn1_wg, ffn1_wu, ffn1_wd, ffn2_wg, ffn2_wu, ffn2_wd,
              w_in, sgu_ln_g, sgu_ln_b, sgu_w, sgu_b, w_out, final_g):
    B, S, D = x.shape
    cos, sin = rope_tables(S, ATT_HEAD_DIM, x.dtype)
    c_act = jax.nn.silu(c)
    h = x
    for l in range(DEPTH):
        mod = c_act @ ada_w[l] + ada_b[l]
        sh1, sc1, g1, sh2, sc2, g2, sh3, sc3, g3 = jnp.split(mod, N_ADA, axis=-1)

        y = modulate(rmsnorm(h, norm_g[l, 0]), sh1, sc1)
        h = h + 0.5 * g1[:, None, :] * swiglu(y, ffn1_wg[l], ffn1_wu[l], ffn1_wd[l])

        y = modulate(rmsnorm(h, norm_g[l, 1]), sh2, sc2)
        proj = y @ w_in[l]
        u_a, v_a, q_b, k_b, v_b = jnp.split(
            proj, [SGU_WIDTH, 2 * SGU_WIDTH, 2 * SGU_WIDTH + ATT_WIDTH, 2 * SGU_WIDTH + 2 * ATT_WIDTH], axis=-1)
        u_a = u_a.reshape(B, S, SGU_HEADS, SGU_HEAD_DIM)
        v_a = v_a.reshape(B, S, SGU_HEADS, SGU_HEAD_DIM)
        out_a = spatial_gating(u_a, v_a, sgu_ln_g[l], sgu_ln_b[l], sgu_w[l], sgu_b[l])
        q_b = apply_rope(q_b.reshape(B, S, ATT_HEADS, ATT_HEAD_DIM), cos, sin)
        k_b = apply_rope(k_b.reshape(B, S, ATT_HEADS, ATT_HEAD_DIM), cos, sin)
        v_b = v_b.reshape(B, S, ATT_HEADS, ATT_HEAD_DIM)
        out_b = dilated_mixture(q_b, k_b, v_b)
        mixed = jnp.concatenate([out_a.reshape(B, S, SGU_WIDTH), out_b.reshape(B, S, ATT_WIDTH)], axis=-1)
        h = h + g2[:, None, :] * (mixed @ w_out[l])

        y = modulate(rmsnorm(h, norm_g[l, 2]), sh3, sc3)
        h = h + 0.5 * g3[:, None, :] * swiglu(y, ffn2_wg[l], ffn2_wu[l], ffn2_wd[l])
    return rmsnorm(h, final_g)
```

```python
import functools
import math

import jax
import jax.numpy as jnp
from jax import lax
from jax.experimental import pallas as pl
from jax.experimental.pallas import tpu as pltpu

D_MODEL = 1024
D_FF = 2816
DEPTH = 2
N_ADA = 9
EPS = 1e-6
SGU_WIDTH = 512
SGU_HEADS = 4
SGU_HEAD_DIM = 128
CHUNK = 128
ATT_WIDTH = 512
ATT_HEADS = 8
ATT_HEAD_DIM = 64
ATT_BLOCK = 128
DILATED_PATTERNS = ((128, 1), (512, 4), (2048, 16))
ROPE_THETA = 10000.0
IN_WIDTH = 2 * SGU_WIDTH + 3 * ATT_WIDTH

LANES = 128
HEAD_PAIRS = ATT_WIDTH // LANES
MASK_BIAS = -1e30
VMEM_LIMIT = 56 * 1024 * 1024

BF16 = jnp.bfloat16
F32 = jnp.float32


def _rms_mod(h, ng, sc, sh):
    ms = jnp.mean(h * h, axis=-1, keepdims=True)
    y = h * lax.rsqrt(ms + EPS) * ng
    return y * (1.0 + sc) + sh


def _ada_kernel(c_ref, w_ref, b_ref, o_ref):
    c = c_ref[...]
    c_act = c * jax.nn.sigmoid(c)
    o_ref[...] = jnp.dot(c_act, w_ref[...], preferred_element_type=F32,
                         precision=lax.Precision.HIGHEST) + b_ref[...]


def _ada(c, ada_w, ada_b):
    B = c.shape[0]
    b4 = ada_b.reshape(DEPTH, N_ADA, 1, D_MODEL)
    return pl.pallas_call(
        _ada_kernel,
        grid=(DEPTH, N_ADA),
        in_specs=[
            pl.BlockSpec((B, D_MODEL), lambda l, j: (0, 0)),
            pl.BlockSpec((None, D_MODEL, D_MODEL), lambda l, j: (l, 0, j)),
            pl.BlockSpec((None, None, 1, D_MODEL), lambda l, j: (l, j, 0, 0)),
        ],
        out_specs=pl.BlockSpec((None, None, B, D_MODEL), lambda l, j: (l, j, 0, 0)),
        out_shape=jax.ShapeDtypeStruct((DEPTH, N_ADA, B, D_MODEL), F32),
        compiler_params=pltpu.CompilerParams(dimension_semantics=("parallel", "parallel")),
        name="ada",
    )(c, ada_w, b4)


def _ffn_kernel(*refs, mix, final, fc):
    if mix:
        (h_ref, oa_ref, ob_ref, g2_ref, woa_ref, wob_ref, sh_ref, sc_ref, g_ref, ng_ref,
         wg_ref, wu_ref, wd_ref, fg_ref, o_ref, acc_ref) = refs
    else:
        (h_ref, sh_ref, sc_ref, g_ref, ng_ref, wg_ref, wu_ref, wd_ref, fg_ref, o_ref, acc_ref) = refs
    h = h_ref[...]
    if mix:
        mixed = jnp.dot(oa_ref[...], woa_ref[...], preferred_element_type=F32)
        mixed = mixed + jnp.dot(ob_ref[...], wob_ref[...], preferred_element_type=F32)
        h = h + g2_ref[...] * mixed
    y = _rms_mod(h, ng_ref[...], sc_ref[...], sh_ref[...]).astype(BF16)
    for i in range(D_FF // fc):
        lo = i * fc
        gate = jnp.dot(y, wg_ref[:, lo:lo + fc], preferred_element_type=F32)
        up = jnp.dot(y, wu_ref[:, lo:lo + fc], preferred_element_type=F32)
        act = (gate * jax.nn.sigmoid(gate) * up).astype(BF16)
        part = jnp.dot(act, wd_ref[lo:lo + fc, :], preferred_element_type=F32)
        if i == 0:
            acc_ref[...] = part
        else:
            acc_ref[...] += part
    out = h + (0.5 * g_ref[...]) * acc_ref[...]
    if final:
        ms = jnp.mean(out * out, axis=-1, keepdims=True)
        out = out * lax.rsqrt(ms + EPS) * fg_ref[...]
    o_ref[...] = out


def _ffn(h, mod_l, ng, wg, wu, wd, fg, *, sub, mix_args=None, final=False, tm=512, fc=1408):
    B, S, _ = h.shape
    mix = mix_args is not None
    row = lambda j: pl.BlockSpec((None, None, 1, D_MODEL), lambda b, i, j=j: (j, b, 0, 0))
    const = lambda shape: pl.BlockSpec(shape, lambda b, i: (0,) * len(shape),
                                       pipeline_mode=pl.Buffered(1))
    tile = lambda w: pl.BlockSpec((None, tm, w), lambda b, i: (b, i, 0))
    args = [h]
    specs = [tile(D_MODEL)]
    if mix:
        oa, ob, woa, wob = mix_args
        args += [oa, ob, mod_l, woa, wob]
        specs += [tile(SGU_WIDTH), tile(ATT_WIDTH), row(5),
                  const((SGU_WIDTH, D_MODEL)), const((ATT_WIDTH, D_MODEL))]
    args += [mod_l, mod_l, mod_l, ng.reshape(1, D_MODEL), wg, wu, wd, fg.reshape(1, D_MODEL)]
    specs += [row(3 * sub), row(3 * sub + 1), row(3 * sub + 2), const((1, D_MODEL)),
              const((D_MODEL, D_FF)), const((D_MODEL, D_FF)), const((D_FF, D_MODEL)),
              const((1, D_MODEL))]
    return pl.pallas_call(
        functools.partial(_ffn_kernel, mix=mix, final=final, fc=fc),
        grid=(B, S // tm),
        in_specs=specs,
        out_specs=tile(D_MODEL),
        out_shape=jax.ShapeDtypeStruct((B, S, D_MODEL), F32),
        scratch_shapes=[pltpu.VMEM((tm, D_MODEL), F32)],
        compiler_params=pltpu.CompilerParams(dimension_semantics=("parallel", "parallel"),
                                             vmem_limit_bytes=VMEM_LIMIT),
        name="ffn_mix" if mix else "ffn",
    )(*args)


def _proj_kernel(h_ref, sh_ref, sc_ref, ng_ref, win_ref, lng_ref, lnb_ref, ws_ref, bs_ref,
                 cos_ref, sina_ref, sinb_ref, oa_ref, q_ref, k_ref, v_ref, *, tm):
    y = _rms_mod(h_ref[...], ng_ref[...], sc_ref[...], sh_ref[...]).astype(BF16)
    proj = jnp.dot(y, win_ref[...], preferred_element_type=F32)

    row = lax.broadcasted_iota(jnp.int32, (CHUNK, CHUNK), 0)
    col = lax.broadcasted_iota(jnp.int32, (CHUNK, CHUNK), 1)
    for hd in range(SGU_HEADS):
        lo = hd * SGU_HEAD_DIM
        u = jax.nn.gelu(proj[:, lo:lo + SGU_HEAD_DIM])
        v = jax.nn.gelu(proj[:, SGU_WIDTH + lo:SGU_WIDTH + lo + SGU_HEAD_DIM])
        mu = jnp.mean(v, axis=-1, keepdims=True)
        var = jnp.mean(jnp.square(v - mu), axis=-1, keepdims=True)
        vn = ((v - mu) * lax.rsqrt(var + EPS)) * lng_ref[hd:hd + 1, :] + lnb_ref[hd:hd + 1, :]
        vn = vn.astype(BF16)
        ws = jnp.where(row >= col, ws_ref[hd], 0.0).astype(BF16)
        for c in range(tm // CHUNK):
            r0 = c * CHUNK
            z = jnp.dot(ws, vn[r0:r0 + CHUNK, :], preferred_element_type=F32) + bs_ref[hd]
            oa_ref[r0:r0 + CHUNK, lo:lo + SGU_HEAD_DIM] = (u[r0:r0 + CHUNK, :] * z).astype(oa_ref.dtype)

    cos = cos_ref[...]
    sina = sina_ref[...]
    sinb = sinb_ref[...]
    quarter = ATT_HEAD_DIM // 2
    scale = 1.0 / math.sqrt(ATT_HEAD_DIM)
    for p in range(HEAD_PAIRS):
        lo = 2 * SGU_WIDTH + p * LANES
        for off, ref, mul in ((0, q_ref, scale), (ATT_WIDTH, k_ref, 1.0)):
            t = proj[:, lo + off:lo + off + LANES]
            ahead = pltpu.roll(t, LANES - quarter, 1)
            behind = pltpu.roll(t, quarter, 1)
            r = t * cos + ahead * sina + behind * sinb
            ref[:, p * LANES:(p + 1) * LANES] = r * mul if mul != 1.0 else r
        v_ref[:, p * LANES:(p + 1) * LANES] = proj[:, lo + 2 * ATT_WIDTH:lo + 2 * ATT_WIDTH + LANES]


def _proj(h, mod_l, ng, win, lng, lnb, ws, bs_b, cos, sina, sinb, *, tm=512):
    B, S, _ = h.shape
    row = lambda j: pl.BlockSpec((None, None, 1, D_MODEL), lambda b, i, j=j: (j, b, 0, 0))
    const = lambda shape: pl.BlockSpec(shape, lambda b, i: (0,) * len(shape),
                                       pipeline_mode=pl.Buffered(1))
    tile = lambda w: pl.BlockSpec((None, tm, w), lambda b, i: (b, i, 0))
    tab = pl.BlockSpec((tm, LANES), lambda b, i: (i, 0))
    return pl.pallas_call(
        functools.partial(_proj_kernel, tm=tm),
        grid=(B, S // tm),
        in_specs=[tile(D_MODEL), row(3), row(4), const((1, D_MODEL)), const((D_MODEL, IN_WIDTH)),
                  const((SGU_HEADS, SGU_HEAD_DIM)), const((SGU_HEADS, SGU_HEAD_DIM)),
                  const((SGU_HEADS, CHUNK, CHUNK)), const((SGU_HEADS, CHUNK, SGU_HEAD_DIM)),
                  tab, tab, tab],
        out_specs=[tile(SGU_WIDTH), tile(ATT_WIDTH), tile(ATT_WIDTH), tile(ATT_WIDTH)],
        out_shape=[jax.ShapeDtypeStruct((B, S, SGU_WIDTH), BF16),
                   jax.ShapeDtypeStruct((B, S, ATT_WIDTH), F32),
                   jax.ShapeDtypeStruct((B, S, ATT_WIDTH), F32),
                   jax.ShapeDtypeStruct((B, S, ATT_WIDTH), F32)],
        compiler_params=pltpu.CompilerParams(dimension_semantics=("parallel", "parallel"),
                                             vmem_limit_bytes=VMEM_LIMIT),
        name="proj",
    )(h, mod_l, mod_l, ng.reshape(1, D_MODEL), win, lng, lnb, ws, bs_b, cos, sina, sinb)


def _attn_kernel(q_ref, k_ref, v_ref, o_ref, qa_s, qb_s, k_s, va_s, vb_s, bias_s, num_s, den_s, mx_s, *, S):
    nblk = S // ATT_BLOCK
    lane = lax.broadcasted_iota(jnp.int32, (1, LANES), 1)
    first_head = lane < ATT_HEAD_DIM

    qi = lax.broadcasted_iota(jnp.int32, (ATT_BLOCK, 2 * ATT_BLOCK), 0)
    ki = lax.broadcasted_iota(jnp.int32, (ATT_BLOCK, 2 * ATT_BLOCK), 1)
    diff = qi + ATT_BLOCK - ki
    band = (diff >= 0) & (diff <= ATT_BLOCK)
    bias_s[0] = jnp.where(band, 0.0, MASK_BIAS)
    bias_s[1] = jnp.where(band & (ki >= ATT_BLOCK), 0.0, MASK_BIAS)

    zpad = jnp.zeros((ATT_BLOCK, LANES), BF16)
    k_s[0:ATT_BLOCK, :] = zpad
    va_s[0:ATT_BLOCK, :] = zpad
    vb_s[0:ATT_BLOCK, :] = zpad

    for br, (window, dil) in enumerate(DILATED_PATTERNS):
        assert window // dil == ATT_BLOCK and S % (dil * ATT_BLOCK) == 0
        L = S // dil
        bpr = L // ATT_BLOCK
        cp = 256
        cpr = L // cp

        def gather(i, carry, dil=dil, L=L, cpr=cpr, cp=cp):
            r = i // cpr
            c = i % cpr
            src = pl.ds(r + dil * cp * c, cp, stride=dil) if dil > 1 else pl.ds(pl.multiple_of(cp * i, cp), cp)
            dst = pl.multiple_of(r * L + c * cp, cp)
            q = q_ref[src, :]
            qa_s[pl.ds(dst, cp), :] = jnp.where(first_head, q, 0.0).astype(BF16)
            qb_s[pl.ds(dst, cp), :] = jnp.where(first_head, 0.0, q).astype(BF16)
            k_s[pl.ds(dst + ATT_BLOCK, cp), :] = k_ref[src, :].astype(BF16)
            v = v_ref[src, :]
            va_s[pl.ds(dst + ATT_BLOCK, cp), :] = jnp.where(first_head, v, 1.0).astype(BF16)
            vb_s[pl.ds(dst + ATT_BLOCK, cp), :] = jnp.where(first_head, 1.0, v).astype(BF16)
            return carry

        lax.fori_loop(0, S // cp, gather, 0)

        def block(m, carry, br=br, dil=dil, bpr=bpr):
            r = m // bpr
            n = m % bpr
            row0 = pl.multiple_of(m * ATT_BLOCK, ATT_BLOCK)
            kb = k_s[pl.ds(row0, 2 * ATT_BLOCK), :]
            bias = bias_s[jnp.where(n == 0, 1, 0)]
            res = []
            mxs = []
            for q_s, v_s in ((qa_s, va_s), (qb_s, vb_s)):
                s = lax.dot_general(q_s[pl.ds(row0, ATT_BLOCK), :], kb, (((1,), (1,)), ((), ())),
                                    preferred_element_type=F32) + bias
                mx = jnp.max(s, axis=-1, keepdims=True)
                p = jnp.exp(s - mx).astype(BF16)
                res.append(jnp.dot(p, v_s[pl.ds(row0, 2 * ATT_BLOCK), :], preferred_element_type=F32))
                mxs.append(mx)
            num = jnp.where(first_head, res[0], res[1])
            den = pltpu.roll(jnp.where(first_head, res[1], res[0]), ATT_HEAD_DIM, 1)
            mxt = jnp.where(first_head, mxs[0], mxs[1])
            if dil > 1:
                dst = pl.ds(r + dil * ATT_BLOCK * n, ATT_BLOCK, stride=dil)
            else:
                dst = pl.ds(row0, ATT_BLOCK)
            num_s[br, dst, :] = num
            den_s[br, dst, :] = den
            mx_s[br, dst, :] = mxt
            return carry

        lax.fori_loop(0, nblk, block, 0)

    cc = 256

    def combine(i, carry):
        rows = pl.ds(pl.multiple_of(i * cc, cc), cc)
        m0, m1, m2 = mx_s[0, rows, :], mx_s[1, rows, :], mx_s[2, rows, :]
        mm = jnp.maximum(jnp.maximum(m0, m1), m2)
        a0, a1, a2 = jnp.exp(m0 - mm), jnp.exp(m1 - mm), jnp.exp(m2 - mm)
        num = a0 * num_s[0, rows, :] + a1 * num_s[1, rows, :] + a2 * num_s[2, rows, :]
        den = a0 * den_s[0, rows, :] + a1 * den_s[1, rows, :] + a2 * den_s[2, rows, :]
        o_ref[rows, :] = (num / den).astype(o_ref.dtype)
        return carry

    lax.fori_loop(0, S // cc, combine, 0)


def _attn(q, k, v):
    B, S, _ = q.shape
    blk = pl.BlockSpec((None, S, LANES), lambda b, p: (b, 0, p))
    return pl.pallas_call(
        functools.partial(_attn_kernel, S=S),
        grid=(B, HEAD_PAIRS),
        in_specs=[blk, blk, blk],
        out_specs=blk,
        out_shape=jax.ShapeDtypeStruct((B, S, ATT_WIDTH), BF16),
        scratch_shapes=[
            pltpu.VMEM((S, LANES), BF16), pltpu.VMEM((S, LANES), BF16),
            pltpu.VMEM((S + ATT_BLOCK, LANES), BF16),
            pltpu.VMEM((S + ATT_BLOCK, LANES), BF16), pltpu.VMEM((S + ATT_BLOCK, LANES), BF16),
            pltpu.VMEM((2, ATT_BLOCK, 2 * ATT_BLOCK), F32),
            pltpu.VMEM((3, S, LANES), F32), pltpu.VMEM((3, S, LANES), F32), pltpu.VMEM((3, S, LANES), F32),
        ],
        compiler_params=pltpu.CompilerParams(dimension_semantics=("parallel", "parallel"),
                                             vmem_limit_bytes=VMEM_LIMIT),
        name="attn",
    )(q, k, v)


def _rope_tables(S):
    inv = ROPE_THETA ** (-jnp.arange(0, ATT_HEAD_DIM, 2, dtype=F32) / ATT_HEAD_DIM)
    ang = jnp.arange(S, dtype=F32)[:, None] * inv[None, :]
    ang = jnp.concatenate([ang, ang, ang, ang], axis=-1)
    cos, sin = jnp.cos(ang), jnp.sin(ang)
    low_half = (jnp.arange(LANES) % ATT_HEAD_DIM) < ATT_HEAD_DIM // 2
    return cos, jnp.where(low_half, -sin, 0.0), jnp.where(low_half, 0.0, sin)


def kernel(x, c, ada_w, ada_b, norm_g, ffn1_wg, ffn1_wu, ffn1_wd, ffn2_wg, ffn2_wu, ffn2_wd,
           w_in, sgu_ln_g, sgu_ln_b, sgu_w, sgu_b, w_out, final_g):
    B, S, _ = x.shape
    mod = _ada(c, ada_w, ada_b).reshape(DEPTH, N_ADA, B, 1, D_MODEL)
    cos, sina, sinb = _rope_tables(S)
    bs_b = jnp.broadcast_to(sgu_b[..., None], (DEPTH, SGU_HEADS, CHUNK, SGU_HEAD_DIM))
    h = x
    for l in range(DEPTH):
        bf = lambda w: w[l].astype(BF16)
        h = _ffn(h, mod[l], norm_g[l, 0], bf(ffn1_wg), bf(ffn1_wu), bf(ffn1_wd), final_g, sub=0)
        oa, q, k, v = _proj(h, mod[l], norm_g[l, 1], bf(w_in), sgu_ln_g[l], sgu_ln_b[l], sgu_w[l], bs_b[l],
                            cos, sina, sinb)
        ob = _attn(q, k, v)
        wo = bf(w_out)
        h = _ffn(h, mod[l], norm_g[l, 2], bf(ffn2_wg), bf(ffn2_wu), bf(ffn2_wd), final_g, sub=2,
                 mix_args=(oa, ob, wo[:SGU_WIDTH], wo[SGU_WIDTH:]), final=(l == DEPTH - 1))
    return h
```

```python
import functools
import math

import jax
import jax.numpy as jnp
from jax import lax
from jax.experimental import pallas as pl
from jax.experimental.pallas import tpu as pltpu

D_MODEL = 1024
D_FF = 2816
DEPTH = 2
N_ADA = 9
EPS = 1e-6
SGU_WIDTH = 512
SGU_HEADS = 4
SGU_HEAD_DIM = 128
CHUNK = 128
ATT_WIDTH = 512
ATT_HEADS = 8
ATT_HEAD_DIM = 64
ATT_BLOCK = 128
DILATED_PATTERNS = ((128, 1), (512, 4), (2048, 16))
ROPE_THETA = 10000.0
IN_WIDTH = 2 * SGU_WIDTH + 3 * ATT_WIDTH

LANES = 128
HEAD_PAIRS = ATT_WIDTH // LANES
MASK_BIAS = -1e30
VMEM_LIMIT = 56 * 1024 * 1024

BF16 = jnp.bfloat16
F32 = jnp.float32


def _rms_mod(h, ng, sc, sh):
    ms = jnp.mean(h * h, axis=-1, keepdims=True)
    y = h * lax.rsqrt(ms + EPS) * ng
    return y * (1.0 + sc) + sh


def _ada_kernel(c_ref, w_ref, b_ref, o_ref):
    c = c_ref[...]
    c_act = c * jax.nn.sigmoid(c)
    o_ref[...] = jnp.dot(c_act, w_ref[...], preferred_element_type=F32,
                         precision=lax.Precision.HIGHEST) + b_ref[...]


def _ada(c, ada_w, ada_b):
    B = c.shape[0]
    b4 = ada_b.reshape(DEPTH, N_ADA, 1, D_MODEL)
    return pl.pallas_call(
        _ada_kernel,
        grid=(DEPTH, N_ADA),
        in_specs=[
            pl.BlockSpec((B, D_MODEL), lambda l, j: (0, 0)),
            pl.BlockSpec((None, D_MODEL, D_MODEL), lambda l, j: (l, 0, j)),
            pl.BlockSpec((None, None, 1, D_MODEL), lambda l, j: (l, j, 0, 0)),
        ],
        out_specs=pl.BlockSpec((None, None, B, D_MODEL), lambda l, j: (l, j, 0, 0)),
        out_shape=jax.ShapeDtypeStruct((DEPTH, N_ADA, B, D_MODEL), F32),
        compiler_params=pltpu.CompilerParams(dimension_semantics=("parallel", "parallel")),
        name="ada",
    )(c, ada_w, b4)


def _ffn_kernel(*refs, mix, final, fc):
    if mix:
        (h_ref, oa_ref, ob_ref, g2_ref, woa_ref, wob_ref, sh_ref, sc_ref, g_ref, ng_ref,
         wg_ref, wu_ref, wd_ref, fg_ref, o_ref, acc_ref) = refs
    else:
        (h_ref, sh_ref, sc_ref, g_ref, ng_ref, wg_ref, wu_ref, wd_ref, fg_ref, o_ref, acc_ref) = refs
    h = h_ref[...]
    if mix:
        mixed = jnp.dot(oa_ref[...], woa_ref[...], preferred_element_type=F32)
        mixed = mixed + jnp.dot(ob_ref[...], wob_ref[...], preferred_element_type=F32)
        h = h + g2_ref[...] * mixed
    y = _rms_mod(h, ng_ref[...], sc_ref[...], sh_ref[...]).astype(BF16)
    for i in range(D_FF // fc):
        lo = i * fc
        gate = jnp.dot(y, wg_ref[:, lo:lo + fc], preferred_element_type=F32)
        up = jnp.dot(y, wu_ref[:, lo:lo + fc], preferred_element_type=F32)
        act = (gate * jax.nn.sigmoid(gate) * up).astype(BF16)
        part = jnp.dot(act, wd_ref[lo:lo + fc, :], preferred_element_type=F32)
        if i == 0:
            acc_ref[...] = part
        else:
            acc_ref[...] += part
    out = h + (0.5 * g_ref[...]) * acc_ref[...]
    if final:
        ms = jnp.mean(out * out, axis=-1, keepdims=True)
        out = out * lax.rsqrt(ms + EPS) * fg_ref[...]
    o_ref[...] = out


def _ffn(h, mod_l, ng, wg, wu, wd, fg, *, sub, mix_args=None, final=False, tm=512, fc=1408):
    B, S, _ = h.shape
    mix = mix_args is not None
    row = lambda j: pl.BlockSpec((None, None, 1, D_MODEL), lambda b, i, j=j: (j, b, 0, 0))
    const = lambda shape: pl.BlockSpec(shape, lambda b, i: (0,) * len(shape),
                                       pipeline_mode=pl.Buffered(1))
    tile = lambda w: pl.BlockSpec((None, tm, w), lambda b, i: (b, i, 0))
    args = [h]
    specs = [tile(D_MODEL)]
    if mix:
        oa, ob, woa, wob = mix_args
        args += [oa, ob, mod_l, woa, wob]
        specs += [tile(SGU_WIDTH), tile(ATT_WIDTH), row(5),
                  const((SGU_WIDTH, D_MODEL)), const((ATT_WIDTH, D_MODEL))]
    args += [mod_l, mod_l, mod_l, ng.reshape(1, D_MODEL), wg, wu, wd, fg.reshape(1, D_MODEL)]
    specs += [row(3 * sub), row(3 * sub + 1), row(3 * sub + 2), const((1, D_MODEL)),
              const((D_MODEL, D_FF)), const((D_MODEL, D_FF)), const((D_FF, D_MODEL)),
              const((1, D_MODEL))]
    return pl.pallas_call(
        functools.partial(_ffn_kernel, mix=mix, final=final, fc=fc),
        grid=(B, S // tm),
        in_specs=specs,
        out_specs=tile(D_MODEL),
        out_shape=jax.ShapeDtypeStruct((B, S, D_MODEL), F32),
        scratch_shapes=[pltpu.VMEM((tm, D_MODEL), F32)],
        compiler_params=pltpu.CompilerParams(dimension_semantics=("parallel", "parallel"),
                                             vmem_limit_bytes=VMEM_LIMIT),
        name="ffn_mix" if mix else "ffn",
    )(*args)


def _proj_kernel(h_ref, sh_ref, sc_ref, ng_ref, win_ref, lng_ref, lnb_ref, ws_ref, bs_ref,
                 cos_ref, sina_ref, sinb_ref, oa_ref, q_ref, k_ref, v_ref, *, tm):
    y = _rms_mod(h_ref[...], ng_ref[...], sc_ref[...], sh_ref[...]).astype(BF16)
    proj = jnp.dot(y, win_ref[...], preferred_element_type=F32)

    row = lax.broadcasted_iota(jnp.int32, (CHUNK, CHUNK), 0)
    col = lax.broadcasted_iota(jnp.int32, (CHUNK, CHUNK), 1)
    for hd in range(SGU_HEADS):
        lo = hd * SGU_HEAD_DIM
        u = jax.nn.gelu(proj[:, lo:lo + SGU_HEAD_DIM])
        v = jax.nn.gelu(proj[:, SGU_WIDTH + lo:SGU_WIDTH + lo + SGU_HEAD_DIM])
        mu = jnp.mean(v, axis=-1, keepdims=True)
        var = jnp.mean(jnp.square(v - mu), axis=-1, keepdims=True)
        vn = ((v - mu) * lax.rsqrt(var + EPS)) * lng_ref[hd:hd + 1, :] + lnb_ref[hd:hd + 1, :]
        vn = vn.astype(BF16)
        ws = jnp.where(row >= col, ws_ref[hd], 0.0).astype(BF16)
        for c in range(tm // CHUNK):
            r0 = c * CHUNK
            z = jnp.dot(ws, vn[r0:r0 + CHUNK, :], preferred_element_type=F32) + bs_ref[hd]
            oa_ref[r0:r0 + CHUNK, lo:lo + SGU_HEAD_DIM] = (u[r0:r0 + CHUNK, :] * z).astype(oa_ref.dtype)

    cos = cos_ref[...]
    sina = sina_ref[...]
    sinb = sinb_ref[...]
    quarter = ATT_HEAD_DIM // 2
    scale = 1.0 / math.sqrt(ATT_HEAD_DIM)
    for p in range(HEAD_PAIRS):
        lo = 2 * SGU_WIDTH + p * LANES
        for off, ref, mul in ((0, q_ref, scale), (ATT_WIDTH, k_ref, 1.0)):
            t = proj[:, lo + off:lo + off + LANES]
            ahead = pltpu.roll(t, LANES - quarter, 1)
            behind = pltpu.roll(t, quarter, 1)
            r = t * cos + ahead * sina + behind * sinb
            ref[:, p * LANES:(p + 1) * LANES] = r * mul if mul != 1.0 else r
        v_ref[:, p * LANES:(p + 1) * LANES] = proj[:, lo + 2 * ATT_WIDTH:lo + 2 * ATT_WIDTH + LANES]


def _proj(h, mod_l, ng, win, lng, lnb, ws, bs_b, cos, sina, sinb, *, tm=512):
    B, S, _ = h.shape
    row = lambda j: pl.BlockSpec((None, None, 1, D_MODEL), lambda b, i, j=j: (j, b, 0, 0))
    const = lambda shape: pl.BlockSpec(shape, lambda b, i: (0,) * len(shape),
                                       pipeline_mode=pl.Buffered(1))
    tile = lambda w: pl.BlockSpec((None, tm, w), lambda b, i: (b, i, 0))
    tab = pl.BlockSpec((tm, LANES), lambda b, i: (i, 0))
    return pl.pallas_call(
        functools.partial(_proj_kernel, tm=tm),
        grid=(B, S // tm),
        in_specs=[tile(D_MODEL), row(3), row(4), const((1, D_MODEL)), const((D_MODEL, IN_WIDTH)),
                  const((SGU_HEADS, SGU_HEAD_DIM)), const((SGU_HEADS, SGU_HEAD_DIM)),
                  const((SGU_HEADS, CHUNK, CHUNK)), const((SGU_HEADS, CHUNK, SGU_HEAD_DIM)),
                  tab, tab, tab],
        out_specs=[tile(SGU_WIDTH), tile(ATT_WIDTH), tile(ATT_WIDTH), tile(ATT_WIDTH)],
        out_shape=[jax.ShapeDtypeStruct((B, S, SGU_WIDTH), BF16),
                   jax.ShapeDtypeStruct((B, S, ATT_WIDTH), F32),
                   jax.ShapeDtypeStruct((B, S, ATT_WIDTH), F32),
                   jax.ShapeDtypeStruct((B, S, ATT_WIDTH), F32)],
        compiler_params=pltpu.CompilerParams(dimension_semantics=("parallel", "parallel"),
                                             vmem_limit_bytes=VMEM_LIMIT),
        name="proj",
    )(h, mod_l, mod_l, ng.reshape(1, D_MODEL), win, lng, lnb, ws, bs_b, cos, sina, sinb)


def _attn_kernel(q_ref, k_ref, v_ref, o_ref, qa_s, qb_s, k_s, wa_s, wb_s, bias_s, s_buf, p_buf, m_buf,
                 num_s, den_s, mx_s, *, S, G):
    nblk = S // ATT_BLOCK
    T = nblk // G
    lane = lax.broadcasted_iota(jnp.int32, (1, LANES), 1)
    first_head = lane < ATT_HEAD_DIM
    nt = (((1,), (1,)), ((), ()))

    qi = lax.broadcasted_iota(jnp.int32, (ATT_BLOCK, 2 * ATT_BLOCK), 0)
    ki = lax.broadcasted_iota(jnp.int32, (ATT_BLOCK, 2 * ATT_BLOCK), 1)
    diff = qi + ATT_BLOCK - ki
    band = (diff >= 0) & (diff <= ATT_BLOCK)
    bias_s[0] = jnp.where(band, 0.0, MASK_BIAS)
    bias_s[1] = jnp.where(band & (ki >= ATT_BLOCK), 0.0, MASK_BIAS)

    rows = S + ATT_BLOCK
    wa_s[:, LANES:] = jnp.broadcast_to(jnp.where(first_head, 1.0, 0.0).astype(BF16), (rows, LANES))
    wb_s[:, LANES:] = jnp.broadcast_to(jnp.where(first_head, 0.0, 1.0).astype(BF16), (rows, LANES))
    zpad = jnp.zeros((ATT_BLOCK, LANES), BF16)
    k_s[0:ATT_BLOCK, :] = zpad
    wa_s[0:ATT_BLOCK, 0:LANES] = zpad
    wb_s[0:ATT_BLOCK, 0:LANES] = zpad

    for br, (window, dil) in enumerate(DILATED_PATTERNS):
        assert window // dil == ATT_BLOCK and S % (dil * ATT_BLOCK) == 0
        L = S // dil
        bpr = L // ATT_BLOCK
        cp = 256
        cpr = L // cp

        def gather(i, carry, dil=dil, L=L, cpr=cpr, cp=cp):
            r = i // cpr
            c = i % cpr
            src = pl.ds(r + dil * cp * c, cp, stride=dil) if dil > 1 else pl.ds(pl.multiple_of(cp * i, cp), cp)
            dst = pl.multiple_of(r * L + c * cp, cp)
            q = q_ref[src, :]
            qa_s[pl.ds(dst, cp), :] = jnp.where(first_head, q, 0.0).astype(BF16)
            qb_s[pl.ds(dst, cp), :] = jnp.where(first_head, 0.0, q).astype(BF16)
            k_s[pl.ds(dst + ATT_BLOCK, cp), :] = k_ref[src, :].astype(BF16)
            v = v_ref[src, :]
            wa_s[pl.ds(dst + ATT_BLOCK, cp), 0:LANES] = jnp.where(first_head, v, 0.0).astype(BF16)
            wb_s[pl.ds(dst + ATT_BLOCK, cp), 0:LANES] = jnp.where(first_head, 0.0, v).astype(BF16)
            return carry

        lax.fori_loop(0, S // cp, gather, 0)

        def scores(t, bpr=bpr):
            for g in range(G):
                m = t * G + g
                row0 = pl.multiple_of(m * ATT_BLOCK, ATT_BLOCK)
                q2 = jnp.concatenate([qa_s[pl.ds(row0, ATT_BLOCK), :], qb_s[pl.ds(row0, ATT_BLOCK), :]], axis=0)
                s = lax.dot_general(q2, k_s[pl.ds(row0, 2 * ATT_BLOCK), :], nt, preferred_element_type=F32)
                bias = bias_s[jnp.where(m % bpr == 0, 1, 0)]
                s_buf[g, 0:ATT_BLOCK, :] = s[0:ATT_BLOCK] + bias
                s_buf[g, ATT_BLOCK:, :] = s[ATT_BLOCK:] + bias

        def softmax():
            rc = 32
            for g in range(G):
                for c in range(2 * ATT_BLOCK // rc):
                    s = s_buf[g, c * rc:(c + 1) * rc, :]
                    mx = jnp.max(s, axis=-1, keepdims=True)
                    p_buf[g, c * rc:(c + 1) * rc, :] = jnp.exp(s - mx).astype(BF16)
                    m_buf[g, c * rc:(c + 1) * rc, :] = jnp.broadcast_to(mx, (rc, LANES))

        def values(t, br=br, dil=dil, bpr=bpr):
            for g in range(G):
                m = t * G + g
                row0 = pl.multiple_of(m * ATT_BLOCK, ATT_BLOCK)
                p2 = jnp.concatenate([p_buf[g, 0:ATT_BLOCK, :], p_buf[g, ATT_BLOCK:, :]], axis=1)
                w = jnp.concatenate([wa_s[pl.ds(row0, 2 * ATT_BLOCK), :], wb_s[pl.ds(row0, 2 * ATT_BLOCK), :]],
                                    axis=0)
                res = jnp.dot(p2, w, preferred_element_type=F32)
                mxt = jnp.where(first_head, m_buf[g, 0:ATT_BLOCK, :], m_buf[g, ATT_BLOCK:, :])
                if dil > 1:
                    dst = pl.ds(m // bpr + dil * ATT_BLOCK * (m % bpr), ATT_BLOCK, stride=dil)
                else:
                    dst = pl.ds(row0, ATT_BLOCK)
                num_s[br, dst, :] = res[:, 0:LANES]
                den_s[br, dst, :] = res[:, LANES:]
                mx_s[br, dst, :] = mxt

        def trip(t, carry):
            values(t - 1)
            softmax()
            scores(t + 1)
            return carry

        scores(0)
        softmax()
        scores(1)
        lax.fori_loop(1, T - 1, trip, 0)
        values(T - 2)
        softmax()
        values(T - 1)

    cc = 256

    def combine(i, carry):
        rows = pl.ds(pl.multiple_of(i * cc, cc), cc)
        m0, m1, m2 = mx_s[0, rows, :], mx_s[1, rows, :], mx_s[2, rows, :]
        mm = jnp.maximum(jnp.maximum(m0, m1), m2)
        a0, a1, a2 = jnp.exp(m0 - mm), jnp.exp(m1 - mm), jnp.exp(m2 - mm)
        num = a0 * num_s[0, rows, :] + a1 * num_s[1, rows, :] + a2 * num_s[2, rows, :]
        den = a0 * den_s[0, rows, :] + a1 * den_s[1, rows, :] + a2 * den_s[2, rows, :]
        o_ref[rows, :] = (num / den).astype(o_ref.dtype)
        return carry

    lax.fori_loop(0, S // cc, combine, 0)


def _attn(q, k, v, *, group=4):
    B, S, _ = q.shape
    blk = pl.BlockSpec((None, S, LANES), lambda b, p: (b, 0, p))
    pad = S + ATT_BLOCK
    return pl.pallas_call(
        functools.partial(_attn_kernel, S=S, G=group),
        grid=(B, HEAD_PAIRS),
        in_specs=[blk, blk, blk],
        out_specs=blk,
        out_shape=jax.ShapeDtypeStruct((B, S, ATT_WIDTH), BF16),
        scratch_shapes=[
            pltpu.VMEM((S, LANES), BF16), pltpu.VMEM((S, LANES), BF16),
            pltpu.VMEM((pad, LANES), BF16),
            pltpu.VMEM((pad, 2 * LANES), BF16), pltpu.VMEM((pad, 2 * LANES), BF16),
            pltpu.VMEM((2, ATT_BLOCK, 2 * ATT_BLOCK), F32),
            pltpu.VMEM((group, 2 * ATT_BLOCK, 2 * ATT_BLOCK), F32),
            pltpu.VMEM((group, 2 * ATT_BLOCK, 2 * ATT_BLOCK), BF16),
            pltpu.VMEM((group, 2 * ATT_BLOCK, LANES), F32),
            pltpu.VMEM((3, S, LANES), F32), pltpu.VMEM((3, S, LANES), F32), pltpu.VMEM((3, S, LANES), F32),
        ],
        compiler_params=pltpu.CompilerParams(dimension_semantics=("parallel", "parallel"),
                                             vmem_limit_bytes=VMEM_LIMIT),
        name="attn",
    )(q, k, v)


def _rope_tables(S):
    inv = ROPE_THETA ** (-jnp.arange(0, ATT_HEAD_DIM, 2, dtype=F32) / ATT_HEAD_DIM)
    ang = jnp.arange(S, dtype=F32)[:, None] * inv[None, :]
    ang = jnp.concatenate([ang, ang, ang, ang], axis=-1)
    cos, sin = jnp.cos(ang), jnp.sin(ang)
    low_half = (jnp.arange(LANES) % ATT_HEAD_DIM) < ATT_HEAD_DIM // 2
    return cos, jnp.where(low_half, -sin, 0.0), jnp.where(low_half, 0.0, sin)


def kernel(x, c, ada_w, ada_b, norm_g, ffn1_wg, ffn1_wu, ffn1_wd, ffn2_wg, ffn2_wu, ffn2_wd,
           w_in, sgu_ln_g, sgu_ln_b, sgu_w, sgu_b, w_out, final_g):
    B, S, _ = x.shape
    mod = _ada(c, ada_w, ada_b).reshape(DEPTH, N_ADA, B, 1, D_MODEL)
    cos, sina, sinb = _rope_tables(S)
    bs_b = jnp.broadcast_to(sgu_b[..., None], (DEPTH, SGU_HEADS, CHUNK, SGU_HEAD_DIM))
    h = x
    for l in range(DEPTH):
        bf = lambda w: w[l].astype(BF16)
        h = _ffn(h, mod[l], norm_g[l, 0], bf(ffn1_wg), bf(ffn1_wu), bf(ffn1_wd), final_g, sub=0)
        oa, q, k, v = _proj(h, mod[l], norm_g[l, 1], bf(w_in), sgu_ln_g[l], sgu_ln_b[l], sgu_w[l], bs_b[l],
                            cos, sina, sinb)
        ob = _attn(q, k, v)
        wo = bf(w_out)
        h = _ffn(h, mod[l], norm_g[l, 2], bf(ffn2_wg), bf(ffn2_wu), bf(ffn2_wd), final_g, sub=2,
                 mix_args=(oa, ob, wo[:SGU_WIDTH], wo[SGU_WIDTH:]), final=(l == DEPTH - 1))
    return h
```

```python
import functools
import math

import jax
import jax.numpy as jnp
from jax import lax
from jax.experimental import pallas as pl
from jax.experimental.pallas import tpu as pltpu

D_MODEL = 1024
D_FF = 2816
DEPTH = 2
N_ADA = 9
EPS = 1e-6
SGU_WIDTH = 512
SGU_HEADS = 4
SGU_HEAD_DIM = 128
CHUNK = 128
ATT_WIDTH = 512
ATT_HEADS = 8
ATT_HEAD_DIM = 64
ATT_BLOCK = 128
DILATED_PATTERNS = ((128, 1), (512, 4), (2048, 16))
ROPE_THETA = 10000.0
IN_WIDTH = 2 * SGU_WIDTH + 3 * ATT_WIDTH

LANES = 128
HEAD_PAIRS = ATT_WIDTH // LANES
MASK_BIAS = -1e30
VMEM_LIMIT = 56 * 1024 * 1024

BF16 = jnp.bfloat16
F32 = jnp.float32


def _rms_mod(h, ng, sc, sh):
    ms = jnp.mean(h * h, axis=-1, keepdims=True)
    y = h * lax.rsqrt(ms + EPS) * ng
    return y * (1.0 + sc) + sh


def _ada_kernel(c_ref, w_ref, b_ref, o_ref):
    c = c_ref[...]
    c_act = c * jax.nn.sigmoid(c)
    o_ref[...] = jnp.dot(c_act, w_ref[...], preferred_element_type=F32,
                         precision=lax.Precision.HIGHEST) + b_ref[...]


def _ada(c, ada_w, ada_b):
    B = c.shape[0]
    b4 = ada_b.reshape(DEPTH, N_ADA, 1, D_MODEL)
    return pl.pallas_call(
        _ada_kernel,
        grid=(DEPTH, N_ADA),
        in_specs=[
            pl.BlockSpec((B, D_MODEL), lambda l, j: (0, 0)),
            pl.BlockSpec((None, D_MODEL, D_MODEL), lambda l, j: (l, 0, j)),
            pl.BlockSpec((None, None, 1, D_MODEL), lambda l, j: (l, j, 0, 0)),
        ],
        out_specs=pl.BlockSpec((None, None, B, D_MODEL), lambda l, j: (l, j, 0, 0)),
        out_shape=jax.ShapeDtypeStruct((DEPTH, N_ADA, B, D_MODEL), F32),
        compiler_params=pltpu.CompilerParams(dimension_semantics=("parallel", "parallel")),
        name="ada",
    )(c, ada_w, b4)


def _ffn_kernel(*refs, mix, final, fc):
    if mix:
        (h_ref, oa_ref, ob_ref, g2_ref, woa_ref, wob_ref, sh_ref, sc_ref, g_ref, ng_ref,
         wg_ref, wu_ref, wd_ref, fg_ref, o_ref, acc_ref) = refs
    else:
        (h_ref, sh_ref, sc_ref, g_ref, ng_ref, wg_ref, wu_ref, wd_ref, fg_ref, o_ref, acc_ref) = refs
    h = h_ref[...]
    if mix:
        mixed = jnp.dot(oa_ref[...], woa_ref[...], preferred_element_type=F32)
        mixed = mixed + jnp.dot(ob_ref[...], wob_ref[...], preferred_element_type=F32)
        h = h + g2_ref[...] * mixed
    y = _rms_mod(h, ng_ref[...], sc_ref[...], sh_ref[...]).astype(BF16)
    assert sum(fc) == D_FF
    for i in range(len(fc)):
        lo, hi = sum(fc[:i]), sum(fc[:i + 1])
        gate = jnp.dot(y, wg_ref[:, lo:hi], preferred_element_type=F32)
        up = jnp.dot(y, wu_ref[:, lo:hi], preferred_element_type=F32)
        act = (gate * jax.nn.sigmoid(gate) * up).astype(BF16)
        part = jnp.dot(act, wd_ref[lo:hi, :], preferred_element_type=F32)
        if i == 0:
            acc_ref[...] = part
        else:
            acc_ref[...] += part
    out = h + (0.5 * g_ref[...]) * acc_ref[...]
    if final:
        ms = jnp.mean(out * out, axis=-1, keepdims=True)
        out = out * lax.rsqrt(ms + EPS) * fg_ref[...]
    o_ref[...] = out


def _ffn(h, mod_l, ng, wg, wu, wd, fg, *, sub, mix_args=None, final=False, tm=512, fc=(1536, 1280)):
    B, S, _ = h.shape
    mix = mix_args is not None
    row = lambda j: pl.BlockSpec((None, None, 1, D_MODEL), lambda b, i, j=j: (j, b, 0, 0))
    const = lambda shape: pl.BlockSpec(shape, lambda b, i: (0,) * len(shape),
                                       pipeline_mode=pl.Buffered(1))
    tile = lambda w: pl.BlockSpec((None, tm, w), lambda b, i: (b, i, 0))
    args = [h]
    specs = [tile(D_MODEL)]
    if mix:
        oa, ob, woa, wob = mix_args
        args += [oa, ob, mod_l, woa, wob]
        specs += [tile(SGU_WIDTH), tile(ATT_WIDTH), row(5),
                  const((SGU_WIDTH, D_MODEL)), const((ATT_WIDTH, D_MODEL))]
    args += [mod_l, mod_l, mod_l, ng.reshape(1, D_MODEL), wg, wu, wd, fg.reshape(1, D_MODEL)]
    specs += [row(3 * sub), row(3 * sub + 1), row(3 * sub + 2), const((1, D_MODEL)),
              const((D_MODEL, D_FF)), const((D_MODEL, D_FF)), const((D_FF, D_MODEL)),
              const((1, D_MODEL))]
    return pl.pallas_call(
        functools.partial(_ffn_kernel, mix=mix, final=final, fc=fc),
        grid=(B, S // tm),
        in_specs=specs,
        out_specs=tile(D_MODEL),
        out_shape=jax.ShapeDtypeStruct((B, S, D_MODEL), F32),
        scratch_shapes=[pltpu.VMEM((tm, D_MODEL), F32)],
        compiler_params=pltpu.CompilerParams(dimension_semantics=("parallel", "parallel"),
                                             vmem_limit_bytes=VMEM_LIMIT),
        name="ffn_mix" if mix else "ffn",
    )(*args)


def _proj_kernel(h_ref, sh_ref, sc_ref, ng_ref, win_ref, lng_ref, lnb_ref, ws_ref, bs_ref,
                 cos_ref, sina_ref, sinb_ref, oa_ref, q_ref, k_ref, v_ref, *, tm):
    y = _rms_mod(h_ref[...], ng_ref[...], sc_ref[...], sh_ref[...]).astype(BF16)
    proj = jnp.dot(y, win_ref[...], preferred_element_type=F32)

    row = lax.broadcasted_iota(jnp.int32, (CHUNK, CHUNK), 0)
    col = lax.broadcasted_iota(jnp.int32, (CHUNK, CHUNK), 1)
    for hd in range(SGU_HEADS):
        lo = hd * SGU_HEAD_DIM
        u = jax.nn.gelu(proj[:, lo:lo + SGU_HEAD_DIM])
        v = jax.nn.gelu(proj[:, SGU_WIDTH + lo:SGU_WIDTH + lo + SGU_HEAD_DIM])
        mu = jnp.mean(v, axis=-1, keepdims=True)
        var = jnp.mean(jnp.square(v - mu), axis=-1, keepdims=True)
        vn = ((v - mu) * lax.rsqrt(var + EPS)) * lng_ref[hd:hd + 1, :] + lnb_ref[hd:hd + 1, :]
        vn = vn.astype(BF16)
        ws = jnp.where(row >= col, ws_ref[hd], 0.0).astype(BF16)
        for c in range(tm // CHUNK):
            r0 = c * CHUNK
            z = jnp.dot(ws, vn[r0:r0 + CHUNK, :], preferred_element_type=F32) + bs_ref[hd]
            oa_ref[r0:r0 + CHUNK, lo:lo + SGU_HEAD_DIM] = (u[r0:r0 + CHUNK, :] * z).astype(oa_ref.dtype)

    cos = cos_ref[...]
    sina = sina_ref[...]
    sinb = sinb_ref[...]
    quarter = ATT_HEAD_DIM // 2
    scale = 1.0 / math.sqrt(ATT_HEAD_DIM)
    for p in range(HEAD_PAIRS):
        lo = 2 * SGU_WIDTH + p * LANES
        for off, ref, mul in ((0, q_ref, scale), (ATT_WIDTH, k_ref, 1.0)):
            t = proj[:, lo + off:lo + off + LANES]
            ahead = pltpu.roll(t, LANES - quarter, 1)
            behind = pltpu.roll(t, quarter, 1)
            r = t * cos + ahead * sina + behind * sinb
            ref[:, p * LANES:(p + 1) * LANES] = r * mul if mul != 1.0 else r
        v_ref[:, p * LANES:(p + 1) * LANES] = proj[:, lo + 2 * ATT_WIDTH:lo + 2 * ATT_WIDTH + LANES]


def _proj(h, mod_l, ng, win, lng, lnb, ws, bs_b, cos, sina, sinb, *, tm=512):
    B, S, _ = h.shape
    row = lambda j: pl.BlockSpec((None, None, 1, D_MODEL), lambda b, i, j=j: (j, b, 0, 0))
    const = lambda shape: pl.BlockSpec(shape, lambda b, i: (0,) * len(shape),
                                       pipeline_mode=pl.Buffered(1))
    tile = lambda w: pl.BlockSpec((None, tm, w), lambda b, i: (b, i, 0))
    tab = pl.BlockSpec((tm, LANES), lambda b, i: (i, 0))
    return pl.pallas_call(
        functools.partial(_proj_kernel, tm=tm),
        grid=(B, S // tm),
        in_specs=[tile(D_MODEL), row(3), row(4), const((1, D_MODEL)), const((D_MODEL, IN_WIDTH)),
                  const((SGU_HEADS, SGU_HEAD_DIM)), const((SGU_HEADS, SGU_HEAD_DIM)),
                  const((SGU_HEADS, CHUNK, CHUNK)), const((SGU_HEADS, CHUNK, SGU_HEAD_DIM)),
                  tab, tab, tab],
        out_specs=[tile(SGU_WIDTH), tile(ATT_WIDTH), tile(ATT_WIDTH), tile(ATT_WIDTH)],
        out_shape=[jax.ShapeDtypeStruct((B, S, SGU_WIDTH), BF16),
                   jax.ShapeDtypeStruct((B, S, ATT_WIDTH), F32),
                   jax.ShapeDtypeStruct((B, S, ATT_WIDTH), F32),
                   jax.ShapeDtypeStruct((B, S, ATT_WIDTH), F32)],
        compiler_params=pltpu.CompilerParams(dimension_semantics=("parallel", "parallel"),
                                             vmem_limit_bytes=VMEM_LIMIT),
        name="proj",
    )(h, mod_l, mod_l, ng.reshape(1, D_MODEL), win, lng, lnb, ws, bs_b, cos, sina, sinb)


def _attn_kernel(q_ref, k_ref, v_ref, o_ref, qa_s, qb_s, k_s, wa_s, wb_s, bias_s, s_buf, p_buf, m_buf,
                 num_s, den_s, mx_s, *, S, G):
    nblk = S // ATT_BLOCK
    T = nblk // G
    lane = lax.broadcasted_iota(jnp.int32, (1, LANES), 1)
    first_head = lane < ATT_HEAD_DIM
    nt = (((1,), (1,)), ((), ()))

    qi = lax.broadcasted_iota(jnp.int32, (ATT_BLOCK, 2 * ATT_BLOCK), 0)
    ki = lax.broadcasted_iota(jnp.int32, (ATT_BLOCK, 2 * ATT_BLOCK), 1)
    diff = qi + ATT_BLOCK - ki
    band = (diff >= 0) & (diff <= ATT_BLOCK)
    bias_s[0] = jnp.where(band, 0.0, MASK_BIAS)
    bias_s[1] = jnp.where(band & (ki >= ATT_BLOCK), 0.0, MASK_BIAS)

    rows = S + ATT_BLOCK
    wa_s[:, LANES:] = jnp.broadcast_to(jnp.where(first_head, 1.0, 0.0).astype(BF16), (rows, LANES))
    wb_s[:, LANES:] = jnp.broadcast_to(jnp.where(first_head, 0.0, 1.0).astype(BF16), (rows, LANES))
    zpad = jnp.zeros((ATT_BLOCK, LANES), BF16)
    k_s[0:ATT_BLOCK, :] = zpad
    wa_s[0:ATT_BLOCK, 0:LANES] = zpad
    wb_s[0:ATT_BLOCK, 0:LANES] = zpad

    for br, (window, dil) in enumerate(DILATED_PATTERNS):
        assert window // dil == ATT_BLOCK and S % (dil * ATT_BLOCK) == 0
        L = S // dil
        bpr = L // ATT_BLOCK
        cp = 256
        cpr = L // cp

        def gather(i, carry, dil=dil, L=L, cpr=cpr, cp=cp):
            r = i // cpr
            c = i % cpr
            src = pl.ds(r + dil * cp * c, cp, stride=dil) if dil > 1 else pl.ds(pl.multiple_of(cp * i, cp), cp)
            dst = pl.multiple_of(r * L + c * cp, cp)
            q = q_ref[src, :]
            qa_s[pl.ds(dst, cp), :] = jnp.where(first_head, q, 0.0).astype(BF16)
            qb_s[pl.ds(dst, cp), :] = jnp.where(first_head, 0.0, q).astype(BF16)
            k_s[pl.ds(dst + ATT_BLOCK, cp), :] = k_ref[src, :].astype(BF16)
            v = v_ref[src, :]
            wa_s[pl.ds(dst + ATT_BLOCK, cp), 0:LANES] = jnp.where(first_head, v, 0.0).astype(BF16)
            wb_s[pl.ds(dst + ATT_BLOCK, cp), 0:LANES] = jnp.where(first_head, 0.0, v).astype(BF16)
            return carry

        lax.fori_loop(0, S // cp, gather, 0)

        def scores(t, bpr=bpr):
            for g in range(G):
                m = t * G + g
                row0 = m * ATT_BLOCK
                q2 = jnp.concatenate([qa_s[pl.ds(row0, ATT_BLOCK), :], qb_s[pl.ds(row0, ATT_BLOCK), :]], axis=0)
                s = lax.dot_general(q2, k_s[pl.ds(row0, 2 * ATT_BLOCK), :], nt, preferred_element_type=F32)
                bias = bias_s[1 if m % bpr == 0 else 0]
                s_buf[g, 0:ATT_BLOCK, :] = s[0:ATT_BLOCK] + bias
                s_buf[g, ATT_BLOCK:, :] = s[ATT_BLOCK:] + bias

        def softmax():
            rc = 32
            for g in range(G):
                for c in range(2 * ATT_BLOCK // rc):
                    s = s_buf[g, c * rc:(c + 1) * rc, :]
                    mx = jnp.max(s, axis=-1, keepdims=True)
                    p_buf[g, c * rc:(c + 1) * rc, :] = jnp.exp(s - mx).astype(BF16)
                    m_buf[g, c * rc:(c + 1) * rc, :] = jnp.broadcast_to(mx, (rc, LANES))

        def values(t, br=br, dil=dil, bpr=bpr):
            for g in range(G):
                m = t * G + g
                row0 = m * ATT_BLOCK
                p2 = jnp.concatenate([p_buf[g, 0:ATT_BLOCK, :], p_buf[g, ATT_BLOCK:, :]], axis=1)
                w = jnp.concatenate([wa_s[pl.ds(row0, 2 * ATT_BLOCK), :], wb_s[pl.ds(row0, 2 * ATT_BLOCK), :]],
                                    axis=0)
                res = jnp.dot(p2, w, preferred_element_type=F32)
                mxt = jnp.where(first_head, m_buf[g, 0:ATT_BLOCK, :], m_buf[g, ATT_BLOCK:, :])
                if dil > 1:
                    dst = pl.ds(m // bpr + dil * ATT_BLOCK * (m % bpr), ATT_BLOCK, stride=dil)
                else:
                    dst = pl.ds(row0, ATT_BLOCK)
                num_s[br, dst, :] = res[:, 0:LANES]
                den_s[br, dst, :] = res[:, LANES:]
                mx_s[br, dst, :] = mxt

        scores(0)
        softmax()
        scores(1)
        for t in range(1, T - 1):
            values(t - 1)
            softmax()
            scores(t + 1)
        values(T - 2)
        softmax()
        values(T - 1)

    cc = 256

    def combine(i, carry):
        rows = pl.ds(pl.multiple_of(i * cc, cc), cc)
        m0, m1, m2 = mx_s[0, rows, :], mx_s[1, rows, :], mx_s[2, rows, :]
        mm = jnp.maximum(jnp.maximum(m0, m1), m2)
        a0, a1, a2 = jnp.exp(m0 - mm), jnp.exp(m1 - mm), jnp.exp(m2 - mm)
        num = a0 * num_s[0, rows, :] + a1 * num_s[1, rows, :] + a2 * num_s[2, rows, :]
        den = a0 * den_s[0, rows, :] + a1 * den_s[1, rows, :] + a2 * den_s[2, rows, :]
        o_ref[rows, :] = (num / den).astype(o_ref.dtype)
        return carry

    lax.fori_loop(0, S // cc, combine, 0)


def _attn(q, k, v, *, group=4):
    B, S, _ = q.shape
    blk = pl.BlockSpec((None, S, LANES), lambda b, p: (b, 0, p))
    pad = S + ATT_BLOCK
    return pl.pallas_call(
        functools.partial(_attn_kernel, S=S, G=group),
        grid=(B, HEAD_PAIRS),
        in_specs=[blk, blk, blk],
        out_specs=blk,
        out_shape=jax.ShapeDtypeStruct((B, S, ATT_WIDTH), BF16),
        scratch_shapes=[
            pltpu.VMEM((S, LANES), BF16), pltpu.VMEM((S, LANES), BF16),
            pltpu.VMEM((pad, LANES), BF16),
            pltpu.VMEM((pad, 2 * LANES), BF16), pltpu.VMEM((pad, 2 * LANES), BF16),
            pltpu.VMEM((2, ATT_BLOCK, 2 * ATT_BLOCK), F32),
            pltpu.VMEM((group, 2 * ATT_BLOCK, 2 * ATT_BLOCK), F32),
            pltpu.VMEM((group, 2 * ATT_BLOCK, 2 * ATT_BLOCK), BF16),
            pltpu.VMEM((group, 2 * ATT_BLOCK, LANES), F32),
            pltpu.VMEM((3, S, LANES), F32), pltpu.VMEM((3, S, LANES), F32), pltpu.VMEM((3, S, LANES), F32),
        ],
        compiler_params=pltpu.CompilerParams(dimension_semantics=("parallel", "parallel"),
                                             vmem_limit_bytes=VMEM_LIMIT),
        name="attn",
    )(q, k, v)


def _rope_tables(S):
    inv = ROPE_THETA ** (-jnp.arange(0, ATT_HEAD_DIM, 2, dtype=F32) / ATT_HEAD_DIM)
    ang = jnp.arange(S, dtype=F32)[:, None] * inv[None, :]
    ang = jnp.concatenate([ang, ang, ang, ang], axis=-1)
    cos, sin = jnp.cos(ang), jnp.sin(ang)
    low_half = (jnp.arange(LANES) % ATT_HEAD_DIM) < ATT_HEAD_DIM // 2
    return cos, jnp.where(low_half, -sin, 0.0), jnp.where(low_half, 0.0, sin)


def kernel(x, c, ada_w, ada_b, norm_g, ffn1_wg, ffn1_wu, ffn1_wd, ffn2_wg, ffn2_wu, ffn2_wd,
           w_in, sgu_ln_g, sgu_ln_b, sgu_w, sgu_b, w_out, final_g):
    B, S, _ = x.shape
    mod = _ada(c, ada_w, ada_b).reshape(DEPTH, N_ADA, B, 1, D_MODEL)
    cos, sina, sinb = _rope_tables(S)
    bs_b = jnp.broadcast_to(sgu_b[..., None], (DEPTH, SGU_HEADS, CHUNK, SGU_HEAD_DIM))
    h = x
    for l in range(DEPTH):
        bf = lambda w: w[l].astype(BF16)
        h = _ffn(h, mod[l], norm_g[l, 0], bf(ffn1_wg), bf(ffn1_wu), bf(ffn1_wd), final_g, sub=0)
        oa, q, k, v = _proj(h, mod[l], norm_g[l, 1], bf(w_in), sgu_ln_g[l], sgu_ln_b[l], sgu_w[l], bs_b[l],
                            cos, sina, sinb)
        ob = _attn(q, k, v)
        wo = bf(w_out)
        h = _ffn(h, mod[l], norm_g[l, 2], bf(ffn2_wg), bf(ffn2_wu), bf(ffn2_wd), final_g, sub=2,
                 mix_args=(oa, ob, wo[:SGU_WIDTH], wo[SGU_WIDTH:]), final=(l == DEPTH - 1))
    return h
```

```python
import functools
import math

import jax
import jax.numpy as jnp
from jax import lax
from jax.experimental import pallas as pl
from jax.experimental.pallas import tpu as pltpu

D_MODEL = 1024
D_FF = 2816
DEPTH = 2
N_ADA = 9
EPS = 1e-6
SGU_WIDTH = 512
SGU_HEADS = 4
SGU_HEAD_DIM = 128
CHUNK = 128
ATT_WIDTH = 512
ATT_HEADS = 8
ATT_HEAD_DIM = 64
ATT_BLOCK = 128
DILATED_PATTERNS = ((128, 1), (512, 4), (2048, 16))
ROPE_THETA = 10000.0
IN_WIDTH = 2 * SGU_WIDTH + 3 * ATT_WIDTH

LANES = 128
HEAD_PAIRS = ATT_WIDTH // LANES
MASK_BIAS = -1e30
VMEM_LIMIT = 56 * 1024 * 1024

BF16 = jnp.bfloat16
F32 = jnp.float32


def _rms_mod(h, ng, sc, sh):
    ms = jnp.mean(h * h, axis=-1, keepdims=True)
    y = h * lax.rsqrt(ms + EPS) * ng
    return y * (1.0 + sc) + sh


def _ada_kernel(c_ref, w_ref, b_ref, o_ref):
    c = c_ref[...]
    c_act = c * jax.nn.sigmoid(c)
    o_ref[...] = jnp.dot(c_act, w_ref[...], preferred_element_type=F32,
                         precision=lax.Precision.HIGHEST) + b_ref[...]


def _ada(c, ada_w, ada_b):
    B = c.shape[0]
    b4 = ada_b.reshape(DEPTH, N_ADA, 1, D_MODEL)
    return pl.pallas_call(
        _ada_kernel,
        grid=(DEPTH, N_ADA),
        in_specs=[
            pl.BlockSpec((B, D_MODEL), lambda l, j: (0, 0)),
            pl.BlockSpec((None, D_MODEL, D_MODEL), lambda l, j: (l, 0, j)),
            pl.BlockSpec((None, None, 1, D_MODEL), lambda l, j: (l, j, 0, 0)),
        ],
        out_specs=pl.BlockSpec((None, None, B, D_MODEL), lambda l, j: (l, j, 0, 0)),
        out_shape=jax.ShapeDtypeStruct((DEPTH, N_ADA, B, D_MODEL), F32),
        compiler_params=pltpu.CompilerParams(dimension_semantics=("parallel", "parallel")),
        name="ada",
    )(c, ada_w, b4)


def _ffn_kernel(*refs, mix, final, fc):
    if mix:
        (h_ref, oa_ref, ob_ref, g2_ref, woa_ref, wob_ref, sh_ref, sc_ref, g_ref, ng_ref,
         wg_ref, wu_ref, wd_ref, fg_ref, o_ref, acc_ref) = refs
    else:
        (h_ref, sh_ref, sc_ref, g_ref, ng_ref, wg_ref, wu_ref, wd_ref, fg_ref, o_ref, acc_ref) = refs
    h = h_ref[...]
    if mix:
        mixed = jnp.dot(oa_ref[...], woa_ref[...], preferred_element_type=F32)
        mixed = mixed + jnp.dot(ob_ref[...].astype(BF16), wob_ref[...], preferred_element_type=F32)
        h = h + g2_ref[...] * mixed
    y = _rms_mod(h, ng_ref[...], sc_ref[...], sh_ref[...]).astype(BF16)
    assert sum(fc) == D_FF
    for i in range(len(fc)):
        lo, hi = sum(fc[:i]), sum(fc[:i + 1])
        gate = jnp.dot(y, wg_ref[:, lo:hi], preferred_element_type=F32)
        up = jnp.dot(y, wu_ref[:, lo:hi], preferred_element_type=F32)
        act = (gate * jax.nn.sigmoid(gate) * up).astype(BF16)
        part = jnp.dot(act, wd_ref[lo:hi, :], preferred_element_type=F32)
        if i == 0:
            acc_ref[...] = part
        else:
            acc_ref[...] += part
    out = h + (0.5 * g_ref[...]) * acc_ref[...]
    if final:
        ms = jnp.mean(out * out, axis=-1, keepdims=True)
        out = out * lax.rsqrt(ms + EPS) * fg_ref[...]
    o_ref[...] = out


def _ffn(h, mod_l, ng, wg, wu, wd, fg, *, sub, mix_args=None, final=False, tm=512, fc=(1536, 1280)):
    B, S, _ = h.shape
    mix = mix_args is not None
    row = lambda j: pl.BlockSpec((None, None, 1, D_MODEL), lambda b, i, j=j: (j, b, 0, 0))
    const = lambda shape: pl.BlockSpec(shape, lambda b, i: (0,) * len(shape),
                                       pipeline_mode=pl.Buffered(1))
    tile = lambda w: pl.BlockSpec((None, tm, w), lambda b, i: (b, i, 0))
    args = [h]
    specs = [tile(D_MODEL)]
    if mix:
        oa, ob, woa, wob = mix_args
        args += [oa, ob, mod_l, woa, wob]
        specs += [tile(SGU_WIDTH), tile(ATT_WIDTH), row(5),
                  const((SGU_WIDTH, D_MODEL)), const((ATT_WIDTH, D_MODEL))]
    args += [mod_l, mod_l, mod_l, ng.reshape(1, D_MODEL), wg, wu, wd, fg.reshape(1, D_MODEL)]
    specs += [row(3 * sub), row(3 * sub + 1), row(3 * sub + 2), const((1, D_MODEL)),
              const((D_MODEL, D_FF)), const((D_MODEL, D_FF)), const((D_FF, D_MODEL)),
              const((1, D_MODEL))]
    return pl.pallas_call(
        functools.partial(_ffn_kernel, mix=mix, final=final, fc=fc),
        grid=(B, S // tm),
        in_specs=specs,
        out_specs=tile(D_MODEL),
        out_shape=jax.ShapeDtypeStruct((B, S, D_MODEL), F32),
        scratch_shapes=[pltpu.VMEM((tm, D_MODEL), F32)],
        compiler_params=pltpu.CompilerParams(dimension_semantics=("parallel", "parallel"),
                                             vmem_limit_bytes=VMEM_LIMIT),
        name="ffn_mix" if mix else "ffn",
    )(*args)


def _proj_kernel(h_ref, sh_ref, sc_ref, ng_ref, win_ref, lng_ref, lnb_ref, ws_ref, bs_ref,
                 cos_ref, sina_ref, sinb_ref, oa_ref, q_ref, k_ref, v_ref, *, tm):
    y = _rms_mod(h_ref[...], ng_ref[...], sc_ref[...], sh_ref[...]).astype(BF16)
    proj = jnp.dot(y, win_ref[...], preferred_element_type=F32)

    row = lax.broadcasted_iota(jnp.int32, (CHUNK, CHUNK), 0)
    col = lax.broadcasted_iota(jnp.int32, (CHUNK, CHUNK), 1)
    for hd in range(SGU_HEADS):
        lo = hd * SGU_HEAD_DIM
        u = jax.nn.gelu(proj[:, lo:lo + SGU_HEAD_DIM])
        v = jax.nn.gelu(proj[:, SGU_WIDTH + lo:SGU_WIDTH + lo + SGU_HEAD_DIM])
        mu = jnp.mean(v, axis=-1, keepdims=True)
        var = jnp.mean(jnp.square(v - mu), axis=-1, keepdims=True)
        vn = ((v - mu) * lax.rsqrt(var + EPS)) * lng_ref[hd:hd + 1, :] + lnb_ref[hd:hd + 1, :]
        vn = vn.astype(BF16)
        ws = jnp.where(row >= col, ws_ref[hd], 0.0).astype(BF16)
        for c in range(tm // CHUNK):
            r0 = c * CHUNK
            z = jnp.dot(ws, vn[r0:r0 + CHUNK, :], preferred_element_type=F32) + bs_ref[hd]
            oa_ref[r0:r0 + CHUNK, lo:lo + SGU_HEAD_DIM] = (u[r0:r0 + CHUNK, :] * z).astype(oa_ref.dtype)

    cos = cos_ref[...]
    sina = sina_ref[...]
    sinb = sinb_ref[...]
    quarter = ATT_HEAD_DIM // 2
    scale = math.log2(math.e) / math.sqrt(ATT_HEAD_DIM)
    for p in range(HEAD_PAIRS):
        lo = 2 * SGU_WIDTH + p * LANES
        for off, ref, mul in ((0, q_ref, scale), (ATT_WIDTH, k_ref, 1.0)):
            t = proj[:, lo + off:lo + off + LANES]
            ahead = pltpu.roll(t, LANES - quarter, 1)
            behind = pltpu.roll(t, quarter, 1)
            r = t * cos + ahead * sina + behind * sinb
            ref[:, p * LANES:(p + 1) * LANES] = r * mul if mul != 1.0 else r
        v_ref[:, p * LANES:(p + 1) * LANES] = proj[:, lo + 2 * ATT_WIDTH:lo + 2 * ATT_WIDTH + LANES]


def _proj(h, mod_l, ng, win, lng, lnb, ws, bs_b, cos, sina, sinb, *, tm=512):
    B, S, _ = h.shape
    row = lambda j: pl.BlockSpec((None, None, 1, D_MODEL), lambda b, i, j=j: (j, b, 0, 0))
    const = lambda shape: pl.BlockSpec(shape, lambda b, i: (0,) * len(shape),
                                       pipeline_mode=pl.Buffered(1))
    tile = lambda w: pl.BlockSpec((None, tm, w), lambda b, i: (b, i, 0))
    tab = pl.BlockSpec((tm, LANES), lambda b, i: (i, 0))
    return pl.pallas_call(
        functools.partial(_proj_kernel, tm=tm),
        grid=(B, S // tm),
        in_specs=[tile(D_MODEL), row(3), row(4), const((1, D_MODEL)), const((D_MODEL, IN_WIDTH)),
                  const((SGU_HEADS, SGU_HEAD_DIM)), const((SGU_HEADS, SGU_HEAD_DIM)),
                  const((SGU_HEADS, CHUNK, CHUNK)), const((SGU_HEADS, CHUNK, SGU_HEAD_DIM)),
                  tab, tab, tab],
        out_specs=[tile(SGU_WIDTH), tile(ATT_WIDTH), tile(ATT_WIDTH), tile(ATT_WIDTH)],
        out_shape=[jax.ShapeDtypeStruct((B, S, SGU_WIDTH), BF16),
                   jax.ShapeDtypeStruct((B, S, ATT_WIDTH), F32),
                   jax.ShapeDtypeStruct((B, S, ATT_WIDTH), F32),
                   jax.ShapeDtypeStruct((B, S, ATT_WIDTH), F32)],
        compiler_params=pltpu.CompilerParams(dimension_semantics=("parallel", "parallel"),
                                             vmem_limit_bytes=VMEM_LIMIT),
        name="proj",
    )(h, mod_l, mod_l, ng.reshape(1, D_MODEL), win, lng, lnb, ws, bs_b, cos, sina, sinb)


def _attn_kernel(q_ref, k_ref, v_ref, o_ref, q4_s, k4_s, v4_s, qa_s, qb_s, k_s, wa_s, wb_s, bias_s,
                 num_p, den_p, mx_p, *, S):
    nblk = S // ATT_BLOCK
    S4 = S // 4
    sub = ATT_BLOCK // 4
    assert S % (16 * ATT_BLOCK) == 0 and nblk >= 3
    lane = lax.broadcasted_iota(jnp.int32, (1, LANES), 1)
    first_head = lane < ATT_HEAD_DIM
    nt = (((1,), (1,)), ((), ()))

    qi = lax.broadcasted_iota(jnp.int32, (ATT_BLOCK, 2 * ATT_BLOCK), 0)
    ki = lax.broadcasted_iota(jnp.int32, (ATT_BLOCK, 2 * ATT_BLOCK), 1)
    for tab, qpos in ((0, qi), (2, 4 * (qi % sub) + qi // sub)):
        diff = qpos + ATT_BLOCK - ki
        band = (diff >= 0) & (diff <= ATT_BLOCK)
        bias_s[tab] = jnp.where(band, 0.0, MASK_BIAS)
        bias_s[tab + 1] = jnp.where(band & (ki >= ATT_BLOCK), 0.0, MASK_BIAS)

    rows = S + ATT_BLOCK
    wa_s[:, LANES:] = jnp.broadcast_to(jnp.where(first_head, 1.0, 0.0).astype(BF16), (rows, LANES))
    wb_s[:, LANES:] = jnp.broadcast_to(jnp.where(first_head, 0.0, 1.0).astype(BF16), (rows, LANES))
    zpad = jnp.zeros((ATT_BLOCK, LANES), BF16)
    k_s[0:ATT_BLOCK, :] = zpad
    wa_s[0:ATT_BLOCK, 0:LANES] = zpad
    wb_s[0:ATT_BLOCK, 0:LANES] = zpad

    cp = 256
    nchunk = S // cp

    def to_mod4(i, carry):
        r4 = i // (S4 // cp)
        c = i % (S4 // cp)
        src = pl.ds(r4 + 4 * cp * c, cp, stride=4)
        dst = pl.ds(pl.multiple_of(i * cp, cp), cp)
        q4_s[dst, :] = q_ref[src, :]
        k4_s[dst, :] = k_ref[src, :]
        v4_s[dst, :] = v_ref[src, :]
        return carry

    lax.fori_loop(0, nchunk, to_mod4, 0)

    def put_operands(dst, q, k, v):
        if q is not None:
            qa_s[pl.ds(dst, cp), :] = jnp.where(first_head, q, 0.0).astype(BF16)
            qb_s[pl.ds(dst, cp), :] = jnp.where(first_head, 0.0, q).astype(BF16)
        k_s[pl.ds(dst + ATT_BLOCK, cp), :] = k.astype(BF16)
        wa_s[pl.ds(dst + ATT_BLOCK, cp), 0:LANES] = jnp.where(first_head, v, 0.0).astype(BF16)
        wb_s[pl.ds(dst + ATT_BLOCK, cp), 0:LANES] = jnp.where(first_head, 0.0, v).astype(BF16)

    def merge(m0, n0, d0, m1, n1, d1):
        mm = jnp.maximum(m0, m1)
        a0, a1 = jnp.exp2(m0 - mm), jnp.exp2(m1 - mm)
        return mm, a0 * n0 + a1 * n1, a0 * d0 + a1 * d1

    def softmax(s):
        mx = jnp.max(s, axis=-1, keepdims=True)
        return jnp.exp2(s - mx).astype(BF16), mx

    for window, dil in reversed(DILATED_PATTERNS):
        assert window // dil == ATT_BLOCK and dil in (1, 4, 16)
        bpr = S // dil // ATT_BLOCK

        if dil == 16:
            def fill(i, carry):
                src = pl.ds((i % 4) * S4 + i // 4, cp, stride=4)
                put_operands(pl.multiple_of(i * cp, cp), q4_s[src, :], k4_s[src, :], v4_s[src, :])
                return carry
        elif dil == 4:
            def fill(i, carry):
                src = pl.ds(pl.multiple_of(i * cp, cp), cp)
                put_operands(pl.multiple_of(i * cp, cp), q4_s[src, :], k4_s[src, :], v4_s[src, :])
                return carry
        else:
            def fill(i, carry):
                src = pl.ds(pl.multiple_of(i * cp, cp), cp)
                put_operands(pl.multiple_of(i * cp, cp), None, k_ref[src, :], v_ref[src, :])
                return carry

        lax.fori_loop(0, nchunk, fill, 0)

        def scores(m, dil=dil, bpr=bpr):
            row0 = m * ATT_BLOCK
            if dil == 1:
                picks = [ref[pl.ds(r4 * S4 + sub * m, sub), :] for ref in (qa_s, qb_s) for r4 in range(4)]
            else:
                picks = [qa_s[pl.ds(row0, ATT_BLOCK), :], qb_s[pl.ds(row0, ATT_BLOCK), :]]
            q2 = jnp.concatenate(picks, axis=0)
            s = lax.dot_general(q2, k_s[pl.ds(row0, 2 * ATT_BLOCK), :], nt, preferred_element_type=F32)
            bias = bias_s[(2 if dil == 1 else 0) + (1 if m % bpr == 0 else 0)]
            return jnp.concatenate([s[0:ATT_BLOCK] + bias, s[ATT_BLOCK:] + bias], axis=0)

        def values(m, p, mxs, dil=dil, bpr=bpr):
            row0 = m * ATT_BLOCK
            p2 = jnp.concatenate([p[0:ATT_BLOCK, :], p[ATT_BLOCK:, :]], axis=1)
            w = jnp.concatenate([wa_s[pl.ds(row0, 2 * ATT_BLOCK), :], wb_s[pl.ds(row0, 2 * ATT_BLOCK), :]], axis=0)
            res = jnp.dot(p2, w, preferred_element_type=F32)
            for c in range(4):
                lo = c * sub
                num, den = res[lo:lo + sub, 0:LANES], res[lo:lo + sub, LANES:]
                mx = jnp.where(first_head, mxs[lo:lo + sub, :], mxs[ATT_BLOCK + lo:ATT_BLOCK + lo + sub, :])
                if dil == 16:
                    r16, n = m // bpr, m % bpr
                    dst = pl.ds((r16 % 4) * S4 + 4 * (ATT_BLOCK * n + lo) + r16 // 4, sub, stride=4)
                elif dil == 4:
                    dst = pl.ds(row0 + lo, sub)
                else:
                    dst = pl.ds(c * S4 + sub * m, sub)
                if dil != 16:
                    mx, num, den = merge(mx_p[dst, :], num_p[dst, :], den_p[dst, :], mx, num, den)
                if dil == 1:
                    o_ref[pl.ds(row0 + c, sub, stride=4), :] = num / den
                else:
                    mx_p[dst, :] = mx
                    num_p[dst, :] = num
                    den_p[dst, :] = den

        s = scores(0)
        p, mxs = softmax(s)
        s = scores(1)
        for m in range(1, nblk - 1):
            values(m - 1, p, mxs)
            p, mxs = softmax(s)
            s = scores(m + 1)
        values(nblk - 2, p, mxs)
        p, mxs = softmax(s)
        values(nblk - 1, p, mxs)


def _attn(q, k, v):
    B, S, _ = q.shape
    blk = pl.BlockSpec((None, S, LANES), lambda b, p: (b, 0, p))
    pad = S + ATT_BLOCK
    plane = pltpu.VMEM((S, LANES), F32)
    return pl.pallas_call(
        functools.partial(_attn_kernel, S=S),
        grid=(B, HEAD_PAIRS),
        in_specs=[blk, blk, blk],
        out_specs=blk,
        out_shape=jax.ShapeDtypeStruct((B, S, ATT_WIDTH), F32),
        scratch_shapes=[
            plane, plane, plane,
            pltpu.VMEM((S, LANES), BF16), pltpu.VMEM((S, LANES), BF16),
            pltpu.VMEM((pad, LANES), BF16),
            pltpu.VMEM((pad, 2 * LANES), BF16), pltpu.VMEM((pad, 2 * LANES), BF16),
            pltpu.VMEM((4, ATT_BLOCK, 2 * ATT_BLOCK), F32),
            plane, plane, plane,
        ],
        compiler_params=pltpu.CompilerParams(dimension_semantics=("parallel", "parallel"),
                                             vmem_limit_bytes=VMEM_LIMIT),
        name="attn",
    )(q, k, v)


def _rope_tables(S):
    inv = ROPE_THETA ** (-jnp.arange(0, ATT_HEAD_DIM, 2, dtype=F32) / ATT_HEAD_DIM)
    ang = jnp.arange(S, dtype=F32)[:, None] * inv[None, :]
    ang = jnp.concatenate([ang, ang, ang, ang], axis=-1)
    cos, sin = jnp.cos(ang), jnp.sin(ang)
    low_half = (jnp.arange(LANES) % ATT_HEAD_DIM) < ATT_HEAD_DIM // 2
    return cos, jnp.where(low_half, -sin, 0.0), jnp.where(low_half, 0.0, sin)


def kernel(x, c, ada_w, ada_b, norm_g, ffn1_wg, ffn1_wu, ffn1_wd, ffn2_wg, ffn2_wu, ffn2_wd,
           w_in, sgu_ln_g, sgu_ln_b, sgu_w, sgu_b, w_out, final_g):
    B, S, _ = x.shape
    mod = _ada(c, ada_w, ada_b).reshape(DEPTH, N_ADA, B, 1, D_MODEL)
    cos, sina, sinb = _rope_tables(S)
    bs_b = jnp.broadcast_to(sgu_b[..., None], (DEPTH, SGU_HEADS, CHUNK, SGU_HEAD_DIM))
    h = x
    for l in range(DEPTH):
        bf = lambda w: w[l].astype(BF16)
        h = _ffn(h, mod[l], norm_g[l, 0], bf(ffn1_wg), bf(ffn1_wu), bf(ffn1_wd), final_g, sub=0)
        oa, q, k, v = _proj(h, mod[l], norm_g[l, 1], bf(w_in), sgu_ln_g[l], sgu_ln_b[l], sgu_w[l], bs_b[l],
                            cos, sina, sinb)
        ob = _attn(q, k, v)
        wo = bf(w_out)
        h = _ffn(h, mod[l], norm_g[l, 2], bf(ffn2_wg), bf(ffn2_wu), bf(ffn2_wd), final_g, sub=2,
                 mix_args=(oa, ob, wo[:SGU_WIDTH], wo[SGU_WIDTH:]), final=(l == DEPTH - 1))
    return h
```

```python
import functools
import math

import jax
import jax.numpy as jnp
from jax import lax
from jax.experimental import pallas as pl
from jax.experimental.pallas import tpu as pltpu

D_MODEL = 1024
D_FF = 2816
DEPTH = 2
N_ADA = 9
ADA_COLS = 3
EPS = 1e-6
SGU_WIDTH = 512
SGU_HEADS = 4
SGU_HEAD_DIM = 128
CHUNK = 128
ATT_WIDTH = 512
ATT_HEADS = 8
ATT_HEAD_DIM = 64
ATT_BLOCK = 128
DILATED_PATTERNS = ((128, 1), (512, 4), (2048, 16))
ROPE_THETA = 10000.0
IN_WIDTH = 2 * SGU_WIDTH + 3 * ATT_WIDTH

FFN_ROWS = 256
LANES = 128
HEAD_PAIRS = ATT_WIDTH // LANES
MASK_BIAS = -1e30
VMEM_LIMIT = 56 * 1024 * 1024

BF16 = jnp.bfloat16
F32 = jnp.float32


def _rms_mod(h, ng, sc, sh):
    ms = jnp.mean(h * h, axis=-1, keepdims=True)
    y = h * lax.rsqrt(ms + EPS) * ng
    return y * (1.0 + sc) + sh


def _ada_kernel(c_ref, w_ref, b_ref, o_ref):
    c = c_ref[...]
    c_act = c * jax.nn.sigmoid(c)
    mod = jnp.dot(c_act, w_ref[...], preferred_element_type=F32, precision=lax.Precision.HIGHEST)
    for j in range(ADA_COLS):
        o_ref[j] = mod[:, j * D_MODEL:(j + 1) * D_MODEL] + b_ref[j]


def _ada(c, ada_w, ada_b):
    B = c.shape[0]
    b4 = ada_b.reshape(DEPTH, N_ADA, 1, D_MODEL)
    return pl.pallas_call(
        _ada_kernel,
        grid=(DEPTH, N_ADA // ADA_COLS),
        in_specs=[
            pl.BlockSpec((B, D_MODEL), lambda l, j: (0, 0)),
            pl.BlockSpec((None, D_MODEL, ADA_COLS * D_MODEL), lambda l, j: (l, 0, j)),
            pl.BlockSpec((None, ADA_COLS, 1, D_MODEL), lambda l, j: (l, j, 0, 0)),
        ],
        out_specs=pl.BlockSpec((None, ADA_COLS, B, D_MODEL), lambda l, j: (l, j, 0, 0)),
        out_shape=jax.ShapeDtypeStruct((DEPTH, N_ADA, B, D_MODEL), F32),
        compiler_params=pltpu.CompilerParams(dimension_semantics=("parallel", "parallel"),
                                             vmem_limit_bytes=VMEM_LIMIT),
        name="ada",
    )(c, ada_w, b4)


def _cast_kernel(x_ref, o_ref):
    o_ref[...] = x_ref[...].astype(o_ref.dtype)


def _to_bf16(w):
    depth, R, C = w.shape
    tr = R // 2
    spec = pl.BlockSpec((None, tr, C), lambda l, i: (l, i, 0))
    return pl.pallas_call(
        _cast_kernel,
        grid=(depth, R // tr),
        in_specs=[spec],
        out_specs=spec,
        out_shape=jax.ShapeDtypeStruct(w.shape, BF16),
        compiler_params=pltpu.CompilerParams(dimension_semantics=("parallel", "parallel"),
                                             vmem_limit_bytes=VMEM_LIMIT),
        name="cast",
    )(w)


def _ffn_kernel(*refs, mix, final, fc):
    if mix:
        (h_ref, oa_ref, ob_ref, g2_ref, woa_ref, wob_ref, sh_ref, sc_ref, g_ref, ng_ref,
         wg_ref, wu_ref, wd_ref, fg_ref, o_ref, acc_ref) = refs
    else:
        (h_ref, sh_ref, sc_ref, g_ref, ng_ref, wg_ref, wu_ref, wd_ref, fg_ref, o_ref, acc_ref) = refs
    assert sum(fc) == D_FF and h_ref.shape[0] % FFN_ROWS == 0
    groups = []
    for r in range(h_ref.shape[0] // FFN_ROWS):
        rows = slice(r * FFN_ROWS, (r + 1) * FFN_ROWS)
        h = h_ref[rows, :]
        if mix:
            mixed = jnp.dot(oa_ref[rows, :], woa_ref[...], preferred_element_type=F32)
            mixed = mixed + jnp.dot(ob_ref[rows, :].astype(BF16), wob_ref[...], preferred_element_type=F32)
            h = h + g2_ref[...] * mixed
        y = _rms_mod(h, ng_ref[...], sc_ref[...], sh_ref[...]).astype(BF16)
        groups.append((rows, h, y))
    for rows, h, y in groups:
        for i in range(len(fc)):
            lo, hi = sum(fc[:i]), sum(fc[:i + 1])
            gate = jnp.dot(y, wg_ref[:, lo:hi], preferred_element_type=F32)
            up = jnp.dot(y, wu_ref[:, lo:hi], preferred_element_type=F32)
            act = (gate * jax.nn.sigmoid(gate) * up).astype(BF16)
            part = jnp.dot(act, wd_ref[lo:hi, :], preferred_element_type=F32)
            if i == 0:
                acc_ref[rows, :] = part
            else:
                acc_ref[rows, :] += part
        out = h + (0.5 * g_ref[...]) * acc_ref[rows, :]
        if final:
            ms = jnp.mean(out * out, axis=-1, keepdims=True)
            out = out * lax.rsqrt(ms + EPS) * fg_ref[...]
        o_ref[rows, :] = out


def _ffn(h, mod_l, ng, wg, wu, wd, fg, *, l, sub, mix_args=None, final=False, tm=1024, fc=(1536, 1280)):
    B, S, _ = h.shape
    mix = mix_args is not None
    row = lambda j: pl.BlockSpec((None, None, 1, D_MODEL), lambda b, i, j=j: (j, b, 0, 0))
    const = lambda shape: pl.BlockSpec(shape, lambda b, i: (0,) * len(shape),
                                       pipeline_mode=pl.Buffered(1))
    layer = lambda shape, r=0: pl.BlockSpec((None,) + shape, lambda b, i: (l, r, 0),
                                            pipeline_mode=pl.Buffered(1))
    tile = lambda w: pl.BlockSpec((None, tm, w), lambda b, i: (b, i, 0))
    args = [h]
    specs = [tile(D_MODEL)]
    if mix:
        oa, ob, wo = mix_args
        args += [oa, ob, mod_l, wo, wo]
        specs += [tile(SGU_WIDTH), tile(ATT_WIDTH), row(5),
                  layer((SGU_WIDTH, D_MODEL), 0), layer((ATT_WIDTH, D_MODEL), 1)]
    args += [mod_l, mod_l, mod_l, ng.reshape(1, D_MODEL), wg, wu, wd, fg.reshape(1, D_MODEL)]
    specs += [row(3 * sub), row(3 * sub + 1), row(3 * sub + 2), const((1, D_MODEL)),
              layer((D_MODEL, D_FF)), layer((D_MODEL, D_FF)), layer((D_FF, D_MODEL)),
              const((1, D_MODEL))]
    return pl.pallas_call(
        functools.partial(_ffn_kernel, mix=mix, final=final, fc=fc),
        grid=(B, S // tm),
        in_specs=specs,
        out_specs=tile(D_MODEL),
        out_shape=jax.ShapeDtypeStruct((B, S, D_MODEL), F32),
        scratch_shapes=[pltpu.VMEM((tm, D_MODEL), F32)],
        compiler_params=pltpu.CompilerParams(dimension_semantics=("parallel", "parallel"),
                                             vmem_limit_bytes=VMEM_LIMIT),
        name="ffn_mix" if mix else "ffn",
    )(*args)


def _proj_kernel(h_ref, sh_ref, sc_ref, ng_ref, win_ref, lng_ref, lnb_ref, ws_ref, bs_ref,
                 cos_ref, sina_ref, sinb_ref, oa_ref, q_ref, k_ref, v_ref, *, tm):
    y = _rms_mod(h_ref[...], ng_ref[...], sc_ref[...], sh_ref[...]).astype(BF16)
    proj = jnp.dot(y, win_ref[...], preferred_element_type=F32)

    row = lax.broadcasted_iota(jnp.int32, (CHUNK, CHUNK), 0)
    col = lax.broadcasted_iota(jnp.int32, (CHUNK, CHUNK), 1)
    for hd in range(SGU_HEADS):
        lo = hd * SGU_HEAD_DIM
        u = jax.nn.gelu(proj[:, lo:lo + SGU_HEAD_DIM])
        v = jax.nn.gelu(proj[:, SGU_WIDTH + lo:SGU_WIDTH + lo + SGU_HEAD_DIM])
        mu = jnp.mean(v, axis=-1, keepdims=True)
        var = jnp.mean(jnp.square(v - mu), axis=-1, keepdims=True)
        vn = ((v - mu) * lax.rsqrt(var + EPS)) * lng_ref[hd:hd + 1, :] + lnb_ref[hd:hd + 1, :]
        vn = vn.astype(BF16)
        ws = jnp.where(row >= col, ws_ref[hd], 0.0).astype(BF16)
        for c in range(tm // CHUNK):
            r0 = c * CHUNK
            z = jnp.dot(ws, vn[r0:r0 + CHUNK, :], preferred_element_type=F32) + bs_ref[hd]
            oa_ref[r0:r0 + CHUNK, lo:lo + SGU_HEAD_DIM] = (u[r0:r0 + CHUNK, :] * z).astype(oa_ref.dtype)

    cos = cos_ref[...]
    sina = sina_ref[...]
    sinb = sinb_ref[...]
    quarter = ATT_HEAD_DIM // 2
    scale = math.log2(math.e) / math.sqrt(ATT_HEAD_DIM)
    for p in range(HEAD_PAIRS):
        lo = 2 * SGU_WIDTH + p * LANES
        for off, ref, mul in ((0, q_ref, scale), (ATT_WIDTH, k_ref, 1.0)):
            t = proj[:, lo + off:lo + off + LANES]
            ahead = pltpu.roll(t, LANES - quarter, 1)
            behind = pltpu.roll(t, quarter, 1)
            r = t * cos + ahead * sina + behind * sinb
            ref[:, p * LANES:(p + 1) * LANES] = r * mul if mul != 1.0 else r
        v_ref[:, p * LANES:(p + 1) * LANES] = proj[:, lo + 2 * ATT_WIDTH:lo + 2 * ATT_WIDTH + LANES]


def _proj(h, mod_l, ng, win, lng, lnb, ws, bs_b, cos, sina, sinb, *, l, tm=1024):
    B, S, _ = h.shape
    row = lambda j: pl.BlockSpec((None, None, 1, D_MODEL), lambda b, i, j=j: (j, b, 0, 0))
    const = lambda shape: pl.BlockSpec(shape, lambda b, i: (0,) * len(shape),
                                       pipeline_mode=pl.Buffered(1))
    tile = lambda w: pl.BlockSpec((None, tm, w), lambda b, i: (b, i, 0))
    tab = pl.BlockSpec((tm, LANES), lambda b, i: (i, 0))
    return pl.pallas_call(
        functools.partial(_proj_kernel, tm=tm),
        grid=(B, S // tm),
        in_specs=[tile(D_MODEL), row(3), row(4), const((1, D_MODEL)),
                  pl.BlockSpec((None, D_MODEL, IN_WIDTH), lambda b, i: (l, 0, 0), pipeline_mode=pl.Buffered(1)),
                  const((SGU_HEADS, SGU_HEAD_DIM)), const((SGU_HEADS, SGU_HEAD_DIM)),
                  const((SGU_HEADS, CHUNK, CHUNK)), const((SGU_HEADS, CHUNK, SGU_HEAD_DIM)),
                  tab, tab, tab],
        out_specs=[tile(SGU_WIDTH), tile(ATT_WIDTH), tile(ATT_WIDTH), tile(ATT_WIDTH)],
        out_shape=[jax.ShapeDtypeStruct((B, S, SGU_WIDTH), BF16),
                   jax.ShapeDtypeStruct((B, S, ATT_WIDTH), F32),
                   jax.ShapeDtypeStruct((B, S, ATT_WIDTH), F32),
                   jax.ShapeDtypeStruct((B, S, ATT_WIDTH), F32)],
        compiler_params=pltpu.CompilerParams(dimension_semantics=("parallel", "parallel"),
                                             vmem_limit_bytes=VMEM_LIMIT),
        name="proj",
    )(h, mod_l, mod_l, ng.reshape(1, D_MODEL), win, lng, lnb, ws, bs_b, cos, sina, sinb)


def _attn_kernel(q_ref, k_ref, v_ref, o_ref, q4_s, k4_s, v4_s, qa_s, qb_s, k_s, wa_s, wb_s, bias_s,
                 num_p, den_p, mx_p, *, S):
    nblk = S // ATT_BLOCK
    S4 = S // 4
    sub = ATT_BLOCK // 4
    assert S % (16 * ATT_BLOCK) == 0 and nblk >= 3
    lane = lax.broadcasted_iota(jnp.int32, (1, LANES), 1)
    first_head = lane < ATT_HEAD_DIM
    nt = (((1,), (1,)), ((), ()))

    qi = lax.broadcasted_iota(jnp.int32, (ATT_BLOCK, 2 * ATT_BLOCK), 0)
    ki = lax.broadcasted_iota(jnp.int32, (ATT_BLOCK, 2 * ATT_BLOCK), 1)
    for tab, qpos in ((0, qi), (2, 4 * (qi % sub) + qi // sub)):
        diff = qpos + ATT_BLOCK - ki
        band = (diff >= 0) & (diff <= ATT_BLOCK)
        bias_s[tab] = jnp.where(band, 0.0, MASK_BIAS)
        bias_s[tab + 1] = jnp.where(band & (ki >= ATT_BLOCK), 0.0, MASK_BIAS)

    rows = S + ATT_BLOCK
    wa_s[:, LANES:] = jnp.broadcast_to(jnp.where(first_head, 1.0, 0.0).astype(BF16), (rows, LANES))
    wb_s[:, LANES:] = jnp.broadcast_to(jnp.where(first_head, 0.0, 1.0).astype(BF16), (rows, LANES))
    zpad = jnp.zeros((ATT_BLOCK, LANES), BF16)
    k_s[0:ATT_BLOCK, :] = zpad
    wa_s[0:ATT_BLOCK, 0:LANES] = zpad
    wb_s[0:ATT_BLOCK, 0:LANES] = zpad

    cp = 256
    nchunk = S // cp

    def to_mod4(i, carry):
        r4 = i // (S4 // cp)
        c = i % (S4 // cp)
        src = pl.ds(r4 + 4 * cp * c, cp, stride=4)
        dst = pl.ds(pl.multiple_of(i * cp, cp), cp)
        q4_s[dst, :] = q_ref[src, :]
        k4_s[dst, :] = k_ref[src, :]
        v4_s[dst, :] = v_ref[src, :]
        return carry

    lax.fori_loop(0, nchunk, to_mod4, 0)

    def put_operands(dst, q, k, v):
        if q is not None:
            qa_s[pl.ds(dst, cp), :] = jnp.where(first_head, q, 0.0).astype(BF16)
            qb_s[pl.ds(dst, cp), :] = jnp.where(first_head, 0.0, q).astype(BF16)
        k_s[pl.ds(dst + ATT_BLOCK, cp), :] = k.astype(BF16)
        wa_s[pl.ds(dst + ATT_BLOCK, cp), 0:LANES] = jnp.where(first_head, v, 0.0).astype(BF16)
        wb_s[pl.ds(dst + ATT_BLOCK, cp), 0:LANES] = jnp.where(first_head, 0.0, v).astype(BF16)

    def merge(m0, n0, d0, m1, n1, d1):
        mm = jnp.maximum(m0, m1)
        a0, a1 = jnp.exp2(m0 - mm), jnp.exp2(m1 - mm)
        return mm, a0 * n0 + a1 * n1, a0 * d0 + a1 * d1

    def softmax(s):
        mx = jnp.max(s, axis=-1, keepdims=True)
        return jnp.exp2(s - mx).astype(BF16), mx

    for window, dil in reversed(DILATED_PATTERNS):
        assert window // dil == ATT_BLOCK and dil in (1, 4, 16)
        bpr = S // dil // ATT_BLOCK

        if dil == 16:
            def fill(i, carry):
                src = pl.ds((i % 4) * S4 + i // 4, cp, stride=4)
                put_operands(pl.multiple_of(i * cp, cp), q4_s[src, :], k4_s[src, :], v4_s[src, :])
                return carry
        elif dil == 4:
            def fill(i, carry):
                src = pl.ds(pl.multiple_of(i * cp, cp), cp)
                put_operands(pl.multiple_of(i * cp, cp), q4_s[src, :], k4_s[src, :], v4_s[src, :])
                return carry
        else:
            def fill(i, carry):
                src = pl.ds(pl.multiple_of(i * cp, cp), cp)
                put_operands(pl.multiple_of(i * cp, cp), None, k_ref[src, :], v_ref[src, :])
                return carry

        lax.fori_loop(0, nchunk, fill, 0)

        def scores(m, dil=dil, bpr=bpr):
            row0 = m * ATT_BLOCK
            if dil == 1:
                picks = [ref[pl.ds(r4 * S4 + sub * m, sub), :] for ref in (qa_s, qb_s) for r4 in range(4)]
            else:
                picks = [qa_s[pl.ds(row0, ATT_BLOCK), :], qb_s[pl.ds(row0, ATT_BLOCK), :]]
            q2 = jnp.concatenate(picks, axis=0)
            s = lax.dot_general(q2, k_s[pl.ds(row0, 2 * ATT_BLOCK), :], nt, preferred_element_type=F32)
            bias = bias_s[(2 if dil == 1 else 0) + (1 if m % bpr == 0 else 0)]
            return jnp.concatenate([s[0:ATT_BLOCK] + bias, s[ATT_BLOCK:] + bias], axis=0)

        def values(m, p, mxs, dil=dil, bpr=bpr):
            row0 = m * ATT_BLOCK
            p2 = jnp.concatenate([p[0:ATT_BLOCK, :], p[ATT_BLOCK:, :]], axis=1)
            w = jnp.concatenate([wa_s[pl.ds(row0, 2 * ATT_BLOCK), :], wb_s[pl.ds(row0, 2 * ATT_BLOCK), :]], axis=0)
            res = jnp.dot(p2, w, preferred_element_type=F32)
            for c in range(4):
                lo = c * sub
                num, den = res[lo:lo + sub, 0:LANES], res[lo:lo + sub, LANES:]
                mx = jnp.where(first_head, mxs[lo:lo + sub, :], mxs[ATT_BLOCK + lo:ATT_BLOCK + lo + sub, :])
                if dil == 16:
                    r16, n = m // bpr, m % bpr
                    dst = pl.ds((r16 % 4) * S4 + 4 * (ATT_BLOCK * n + lo) + r16 // 4, sub, stride=4)
                elif dil == 4:
                    dst = pl.ds(row0 + lo, sub)
                else:
                    dst = pl.ds(c * S4 + sub * m, sub)
                if dil != 16:
                    mx, num, den = merge(mx_p[dst, :], num_p[dst, :], den_p[dst, :], mx, num, den)
                if dil == 1:
                    o_ref[pl.ds(row0 + c, sub, stride=4), :] = num / den
                else:
                    mx_p[dst, :] = mx
                    num_p[dst, :] = num
                    den_p[dst, :] = den

        s = scores(0)
        p, mxs = softmax(s)
        s = scores(1)
        for m in range(1, nblk - 1):
            values(m - 1, p, mxs)
            p, mxs = softmax(s)
            s = scores(m + 1)
        values(nblk - 2, p, mxs)
        p, mxs = softmax(s)
        values(nblk - 1, p, mxs)


def _attn(q, k, v):
    B, S, _ = q.shape
    blk = pl.BlockSpec((None, S, LANES), lambda b, p: (b, 0, p))
    pad = S + ATT_BLOCK
    plane = pltpu.VMEM((S, LANES), F32)
    return pl.pallas_call(
        functools.partial(_attn_kernel, S=S),
        grid=(B, HEAD_PAIRS),
        in_specs=[blk, blk, blk],
        out_specs=blk,
        out_shape=jax.ShapeDtypeStruct((B, S, ATT_WIDTH), F32),
        scratch_shapes=[
            plane, plane, plane,
            pltpu.VMEM((S, LANES), BF16), pltpu.VMEM((S, LANES), BF16),
            pltpu.VMEM((pad, LANES), BF16),
            pltpu.VMEM((pad, 2 * LANES), BF16), pltpu.VMEM((pad, 2 * LANES), BF16),
            pltpu.VMEM((4, ATT_BLOCK, 2 * ATT_BLOCK), F32),
            plane, plane, plane,
        ],
        compiler_params=pltpu.CompilerParams(dimension_semantics=("parallel", "parallel"),
                                             vmem_limit_bytes=VMEM_LIMIT),
        name="attn",
    )(q, k, v)


def _rope_tables(S):
    inv = ROPE_THETA ** (-jnp.arange(0, ATT_HEAD_DIM, 2, dtype=F32) / ATT_HEAD_DIM)
    ang = jnp.arange(S, dtype=F32)[:, None] * inv[None, :]
    ang = jnp.concatenate([ang, ang, ang, ang], axis=-1)
    cos, sin = jnp.cos(ang), jnp.sin(ang)
    low_half = (jnp.arange(LANES) % ATT_HEAD_DIM) < ATT_HEAD_DIM // 2
    return cos, jnp.where(low_half, -sin, 0.0), jnp.where(low_half, 0.0, sin)


def kernel(x, c, ada_w, ada_b, norm_g, ffn1_wg, ffn1_wu, ffn1_wd, ffn2_wg, ffn2_wu, ffn2_wd,
           w_in, sgu_ln_g, sgu_ln_b, sgu_w, sgu_b, w_out, final_g):
    B, S, _ = x.shape
    mod = _ada(c, ada_w, ada_b).reshape(DEPTH, N_ADA, B, 1, D_MODEL)
    cos, sina, sinb = _rope_tables(S)
    bs_b = jnp.broadcast_to(sgu_b[..., None], (DEPTH, SGU_HEADS, CHUNK, SGU_HEAD_DIM))
    wg1, wu1, wd1, wg2, wu2, wd2, win, wo = [
        _to_bf16(w) for w in (ffn1_wg, ffn1_wu, ffn1_wd, ffn2_wg, ffn2_wu, ffn2_wd, w_in, w_out)]
    h = x
    for l in range(DEPTH):
        h = _ffn(h, mod[l], norm_g[l, 0], wg1, wu1, wd1, final_g, l=l, sub=0)
        oa, q, k, v = _proj(h, mod[l], norm_g[l, 1], win, sgu_ln_g[l], sgu_ln_b[l], sgu_w[l], bs_b[l],
                            cos, sina, sinb, l=l)
        ob = _attn(q, k, v)
        h = _ffn(h, mod[l], norm_g[l, 2], wg2, wu2, wd2, final_g, l=l, sub=2,
                 mix_args=(oa, ob, wo), final=(l == DEPTH - 1))
    return h
```

```python
import functools
import math

import jax
import jax.numpy as jnp
from jax import lax
from jax.experimental import pallas as pl
from jax.experimental.pallas import tpu as pltpu

D_MODEL = 1024
D_FF = 2816
DEPTH = 2
N_ADA = 9
ADA_COLS = 3
EPS = 1e-6
SGU_WIDTH = 512
SGU_HEADS = 4
SGU_HEAD_DIM = 128
CHUNK = 128
ATT_WIDTH = 512
ATT_HEADS = 8
ATT_HEAD_DIM = 64
ATT_BLOCK = 128
DILATED_PATTERNS = ((128, 1), (512, 4), (2048, 16))
ROPE_THETA = 10000.0
IN_WIDTH = 2 * SGU_WIDTH + 3 * ATT_WIDTH

FFN_ROWS = 256
LANES = 128
HEAD_PAIRS = ATT_WIDTH // LANES
MASK_BIAS = -1e30
VMEM_LIMIT = 56 * 1024 * 1024

BF16 = jnp.bfloat16
F32 = jnp.float32


def _rms_mod(h, ng, sc, sh):
    ms = jnp.mean(h * h, axis=-1, keepdims=True)
    y = h * lax.rsqrt(ms + EPS) * ng
    return y * (1.0 + sc) + sh


def _ada_kernel(c_ref, w_ref, b_ref, o_ref):
    c = c_ref[...]
    c_act = c * jax.nn.sigmoid(c)
    mod = jnp.dot(c_act, w_ref[...], preferred_element_type=F32, precision=lax.Precision.HIGHEST)
    for j in range(ADA_COLS):
        o_ref[j] = mod[:, j * D_MODEL:(j + 1) * D_MODEL] + b_ref[j]


def _ada(c, ada_w, ada_b):
    B = c.shape[0]
    b4 = ada_b.reshape(DEPTH, N_ADA, 1, D_MODEL)
    return pl.pallas_call(
        _ada_kernel,
        grid=(DEPTH, N_ADA // ADA_COLS),
        in_specs=[
            pl.BlockSpec((B, D_MODEL), lambda l, j: (0, 0)),
            pl.BlockSpec((None, D_MODEL, ADA_COLS * D_MODEL), lambda l, j: (l, 0, j)),
            pl.BlockSpec((None, ADA_COLS, 1, D_MODEL), lambda l, j: (l, j, 0, 0)),
        ],
        out_specs=pl.BlockSpec((None, ADA_COLS, B, D_MODEL), lambda l, j: (l, j, 0, 0)),
        out_shape=jax.ShapeDtypeStruct((DEPTH, N_ADA, B, D_MODEL), F32),
        compiler_params=pltpu.CompilerParams(dimension_semantics=("parallel", "parallel"),
                                             vmem_limit_bytes=VMEM_LIMIT),
        name="ada",
    )(c, ada_w, b4)


def _cast_kernel(x_ref, o_ref):
    o_ref[...] = x_ref[...].astype(o_ref.dtype)


def _to_bf16(w):
    depth, R, C = w.shape
    tr = R // 2
    spec = pl.BlockSpec((None, tr, C), lambda l, i: (l, i, 0))
    return pl.pallas_call(
        _cast_kernel,
        grid=(depth, R // tr),
        in_specs=[spec],
        out_specs=spec,
        out_shape=jax.ShapeDtypeStruct(w.shape, BF16),
        compiler_params=pltpu.CompilerParams(dimension_semantics=("parallel", "parallel"),
                                             vmem_limit_bytes=VMEM_LIMIT),
        name="cast",
    )(w)


def _ffn_kernel(*refs, mix, final, fc):
    if mix:
        (h_ref, oa_ref, ob_ref, g2_ref, woa_ref, wob_ref, sh_ref, sc_ref, g_ref, ng_ref,
         wg_ref, wu_ref, wd_ref, fg_ref, o_ref, acc_ref) = refs
    else:
        (h_ref, sh_ref, sc_ref, g_ref, ng_ref, wg_ref, wu_ref, wd_ref, fg_ref, o_ref, acc_ref) = refs
    assert sum(fc) == D_FF and h_ref.shape[0] % FFN_ROWS == 0
    groups = []
    for r in range(h_ref.shape[0] // FFN_ROWS):
        rows = slice(r * FFN_ROWS, (r + 1) * FFN_ROWS)
        h = h_ref[rows, :]
        if mix:
            mixed = jnp.dot(oa_ref[rows, :], woa_ref[...], preferred_element_type=F32)
            mixed = mixed + jnp.dot(ob_ref[rows, :].astype(BF16), wob_ref[...], preferred_element_type=F32)
            h = h + g2_ref[...] * mixed
        y = _rms_mod(h, ng_ref[...], sc_ref[...], sh_ref[...]).astype(BF16)
        groups.append((rows, h, y))
    for rows, h, y in groups:
        for i in range(len(fc)):
            lo, hi = sum(fc[:i]), sum(fc[:i + 1])
            gate = jnp.dot(y, wg_ref[:, lo:hi], preferred_element_type=F32)
            up = jnp.dot(y, wu_ref[:, lo:hi], preferred_element_type=F32)
            act = (gate * jax.nn.sigmoid(gate) * up).astype(BF16)
            part = jnp.dot(act, wd_ref[lo:hi, :], preferred_element_type=F32)
            if i == 0:
                acc_ref[rows, :] = part
            else:
                acc_ref[rows, :] += part
        out = h + (0.5 * g_ref[...]) * acc_ref[rows, :]
        if final:
            ms = jnp.mean(out * out, axis=-1, keepdims=True)
            out = out * lax.rsqrt(ms + EPS) * fg_ref[...]
        o_ref[rows, :] = out


def _ffn(h, mod_l, ng, wg, wu, wd, fg, *, l, sub, mix_args=None, final=False, tm=1024, fc=(1536, 1280)):
    B, S, _ = h.shape
    mix = mix_args is not None
    row = lambda j: pl.BlockSpec((None, None, 1, D_MODEL), lambda b, i, j=j: (j, b, 0, 0))
    const = lambda shape: pl.BlockSpec(shape, lambda b, i: (0,) * len(shape),
                                       pipeline_mode=pl.Buffered(1))
    layer = lambda shape, r=0: pl.BlockSpec((None,) + shape, lambda b, i: (l, r, 0),
                                            pipeline_mode=pl.Buffered(1))
    tile = lambda w: pl.BlockSpec((None, tm, w), lambda b, i: (b, i, 0))
    args = [h]
    specs = [tile(D_MODEL)]
    if mix:
        oa, ob, wo = mix_args
        args += [oa, ob, mod_l, wo, wo]
        specs += [tile(SGU_WIDTH), tile(ATT_WIDTH), row(5),
                  layer((SGU_WIDTH, D_MODEL), 0), layer((ATT_WIDTH, D_MODEL), 1)]
    args += [mod_l, mod_l, mod_l, ng.reshape(1, D_MODEL), wg, wu, wd, fg.reshape(1, D_MODEL)]
    specs += [row(3 * sub), row(3 * sub + 1), row(3 * sub + 2), const((1, D_MODEL)),
              layer((D_MODEL, D_FF)), layer((D_MODEL, D_FF)), layer((D_FF, D_MODEL)),
              const((1, D_MODEL))]
    return pl.pallas_call(
        functools.partial(_ffn_kernel, mix=mix, final=final, fc=fc),
        grid=(B, S // tm),
        in_specs=specs,
        out_specs=tile(D_MODEL),
        out_shape=jax.ShapeDtypeStruct((B, S, D_MODEL), F32),
        scratch_shapes=[pltpu.VMEM((tm, D_MODEL), F32)],
        compiler_params=pltpu.CompilerParams(dimension_semantics=("parallel", "parallel"),
                                             vmem_limit_bytes=VMEM_LIMIT),
        name="ffn_mix" if mix else "ffn",
    )(*args)


def _proj_kernel(h_ref, sh_ref, sc_ref, ng_ref, win_ref, lng_ref, lnb_ref, ws_ref, bs_ref,
                 cos_ref, sina_ref, sinb_ref, oa_ref, q_ref, k_ref, v_ref, nat_s, *, tm):
    y = _rms_mod(h_ref[...], ng_ref[...], sc_ref[...], sh_ref[...]).astype(BF16)
    proj = jnp.dot(y, win_ref[...], preferred_element_type=F32)

    row = lax.broadcasted_iota(jnp.int32, (CHUNK, CHUNK), 0)
    col = lax.broadcasted_iota(jnp.int32, (CHUNK, CHUNK), 1)
    for hd in range(SGU_HEADS):
        lo = hd * SGU_HEAD_DIM
        u = jax.nn.gelu(proj[:, lo:lo + SGU_HEAD_DIM])
        v = jax.nn.gelu(proj[:, SGU_WIDTH + lo:SGU_WIDTH + lo + SGU_HEAD_DIM])
        mu = jnp.mean(v, axis=-1, keepdims=True)
        var = jnp.mean(jnp.square(v - mu), axis=-1, keepdims=True)
        vn = ((v - mu) * lax.rsqrt(var + EPS)) * lng_ref[hd:hd + 1, :] + lnb_ref[hd:hd + 1, :]
        vn = vn.astype(BF16)
        ws = jnp.where(row >= col, ws_ref[hd], 0.0).astype(BF16)
        for c in range(tm // CHUNK):
            r0 = c * CHUNK
            z = jnp.dot(ws, vn[r0:r0 + CHUNK, :], preferred_element_type=F32) + bs_ref[hd]
            oa_ref[r0:r0 + CHUNK, lo:lo + SGU_HEAD_DIM] = (u[r0:r0 + CHUNK, :] * z).astype(oa_ref.dtype)

    cos = cos_ref[...]
    sina = sina_ref[...]
    sinb = sinb_ref[...]
    quarter = ATT_HEAD_DIM // 2
    scale = math.log2(math.e) / math.sqrt(ATT_HEAD_DIM)
    for p in range(HEAD_PAIRS):
        lo = 2 * SGU_WIDTH + p * LANES
        for t, mul in ((0, scale), (1, 1.0)):
            x = proj[:, lo + t * ATT_WIDTH:lo + t * ATT_WIDTH + LANES]
            ahead = pltpu.roll(x, LANES - quarter, 1)
            behind = pltpu.roll(x, quarter, 1)
            r = x * cos + ahead * sina + behind * sinb
            nat_s[t * HEAD_PAIRS + p] = r * mul if mul != 1.0 else r
        nat_s[2 * HEAD_PAIRS + p] = proj[:, lo + 2 * ATT_WIDTH:lo + 2 * ATT_WIDTH + LANES]
    for t, ref in enumerate((q_ref, k_ref, v_ref)):
        for p in range(HEAD_PAIRS):
            for r4 in range(4):
                ref[r4, :, p * LANES:(p + 1) * LANES] = nat_s[t * HEAD_PAIRS + p, pl.ds(r4, tm // 4, stride=4), :]


def _proj(h, mod_l, ng, win, lng, lnb, ws, bs_b, cos, sina, sinb, *, l, tm=1024):
    B, S, _ = h.shape
    row = lambda j: pl.BlockSpec((None, None, 1, D_MODEL), lambda b, i, j=j: (j, b, 0, 0))
    const = lambda shape: pl.BlockSpec(shape, lambda b, i: (0,) * len(shape),
                                       pipeline_mode=pl.Buffered(1))
    tile = lambda w: pl.BlockSpec((None, tm, w), lambda b, i: (b, i, 0))
    tab = pl.BlockSpec((tm, LANES), lambda b, i: (i, 0))
    mod4 = pl.BlockSpec((None, 4, tm // 4, ATT_WIDTH), lambda b, i: (b, 0, i, 0))
    return pl.pallas_call(
        functools.partial(_proj_kernel, tm=tm),
        grid=(B, S // tm),
        in_specs=[tile(D_MODEL), row(3), row(4), const((1, D_MODEL)),
                  pl.BlockSpec((None, D_MODEL, IN_WIDTH), lambda b, i: (l, 0, 0), pipeline_mode=pl.Buffered(1)),
                  const((SGU_HEADS, SGU_HEAD_DIM)), const((SGU_HEADS, SGU_HEAD_DIM)),
                  const((SGU_HEADS, CHUNK, CHUNK)), const((SGU_HEADS, CHUNK, SGU_HEAD_DIM)),
                  tab, tab, tab],
        out_specs=[tile(SGU_WIDTH), mod4, mod4, mod4],
        out_shape=[jax.ShapeDtypeStruct((B, S, SGU_WIDTH), BF16)]
        + [jax.ShapeDtypeStruct((B, 4, S // 4, ATT_WIDTH), F32)] * 3,
        scratch_shapes=[pltpu.VMEM((3 * HEAD_PAIRS, tm, LANES), F32)],
        compiler_params=pltpu.CompilerParams(dimension_semantics=("parallel", "parallel"),
                                             vmem_limit_bytes=VMEM_LIMIT),
        name="proj",
    )(h, mod_l, mod_l, ng.reshape(1, D_MODEL), win, lng, lnb, ws, bs_b, cos, sina, sinb)


def _attn_kernel(q_ref, k_ref, v_ref, o_ref, qa16, qb16, k16, wa16, wb16, qa4, qb4, k4, wa4, wb4, bias_s,
                 num_p, den_p, mx_p, *, S):
    nblk = S // ATT_BLOCK
    S4 = S // 4
    sub = ATT_BLOCK // 4
    assert S % (16 * ATT_BLOCK) == 0 and nblk >= 3
    lane = lax.broadcasted_iota(jnp.int32, (1, LANES), 1)
    first_head = lane < ATT_HEAD_DIM
    nt = (((1,), (1,)), ((), ()))

    qi = lax.broadcasted_iota(jnp.int32, (ATT_BLOCK, 2 * ATT_BLOCK), 0)
    ki = lax.broadcasted_iota(jnp.int32, (ATT_BLOCK, 2 * ATT_BLOCK), 1)
    for tab, qpos, kpos in ((0, qi, ki), (2, 4 * (qi % sub) + qi // sub, 4 * (ki % (2 * sub)) + ki // (2 * sub))):
        diff = qpos + ATT_BLOCK - kpos
        band = (diff >= 0) & (diff <= ATT_BLOCK)
        bias_s[tab] = jnp.where(band, 0.0, MASK_BIAS)
        bias_s[tab + 1] = jnp.where(band & (kpos >= ATT_BLOCK), 0.0, MASK_BIAS)

    rows = S + ATT_BLOCK
    zpad = jnp.zeros((ATT_BLOCK, LANES), BF16)
    for k_s, wa_s, wb_s in ((k16, wa16, wb16), (k4, wa4, wb4)):
        wa_s[:, LANES:] = jnp.broadcast_to(jnp.where(first_head, 1.0, 0.0).astype(BF16), (rows, LANES))
        wb_s[:, LANES:] = jnp.broadcast_to(jnp.where(first_head, 0.0, 1.0).astype(BF16), (rows, LANES))
        k_s[0:ATT_BLOCK, :] = zpad
        wa_s[0:ATT_BLOCK, 0:LANES] = zpad
        wb_s[0:ATT_BLOCK, 0:LANES] = zpad

    cp = 256
    nchunk = S // cp

    def put_operands(ops, dst, q, k, v):
        qa_s, qb_s, k_s, wa_s, wb_s = ops
        qa_s[pl.ds(dst, cp), :] = jnp.where(first_head, q, 0.0).astype(BF16)
        qb_s[pl.ds(dst, cp), :] = jnp.where(first_head, 0.0, q).astype(BF16)
        k_s[pl.ds(dst + ATT_BLOCK, cp), :] = k.astype(BF16)
        wa_s[pl.ds(dst + ATT_BLOCK, cp), 0:LANES] = jnp.where(first_head, v, 0.0).astype(BF16)
        wb_s[pl.ds(dst + ATT_BLOCK, cp), 0:LANES] = jnp.where(first_head, 0.0, v).astype(BF16)

    def key_rows(ref, m, dil):
        if dil == 1:
            return jnp.concatenate(
                [ref[pl.ds(ATT_BLOCK + r4 * S4 + sub * (m - 1), 2 * sub), :] for r4 in range(4)], axis=0)
        return ref[pl.ds(m * ATT_BLOCK, 2 * ATT_BLOCK), :]

    def merge(m0, n0, d0, m1, n1, d1):
        mm = jnp.maximum(m0, m1)
        a0, a1 = jnp.exp2(m0 - mm), jnp.exp2(m1 - mm)
        return mm, a0 * n0 + a1 * n1, a0 * d0 + a1 * d1

    def softmax(s):
        mx = jnp.max(s, axis=-1, keepdims=True)
        return jnp.exp2(s - mx).astype(BF16), mx

    for window, dil in reversed(DILATED_PATTERNS):
        assert window // dil == ATT_BLOCK and dil in (1, 4, 16)
        bpr = S // dil // ATT_BLOCK
        ops = (qa16, qb16, k16, wa16, wb16) if dil == 16 else (qa4, qb4, k4, wa4, wb4)
        qa_s, qb_s, k_s, wa_s, wb_s = ops

        if dil == 16:
            for i in range(nchunk):
                src = pl.ds(i // 4, cp, stride=4)
                put_operands(ops, i * cp, q_ref[i % 4, src, :], k_ref[i % 4, src, :], v_ref[i % 4, src, :])
        elif dil == 4:
            for i in range(nchunk):
                src = pl.ds((i % (S4 // cp)) * cp, cp)
                r4 = i // (S4 // cp)
                put_operands(ops, i * cp, q_ref[r4, src, :], k_ref[r4, src, :], v_ref[r4, src, :])

        def scores(m, dil=dil, bpr=bpr, qa_s=qa_s, qb_s=qb_s, k_s=k_s):
            row0 = m * ATT_BLOCK
            if dil == 1:
                picks = [ref[pl.ds(r4 * S4 + sub * m, sub), :] for ref in (qa_s, qb_s) for r4 in range(4)]
            else:
                picks = [qa_s[pl.ds(row0, ATT_BLOCK), :], qb_s[pl.ds(row0, ATT_BLOCK), :]]
            q2 = jnp.concatenate(picks, axis=0)
            s = lax.dot_general(q2, key_rows(k_s, m, dil), nt, preferred_element_type=F32)
            bias = bias_s[(2 if dil == 1 else 0) + (1 if m % bpr == 0 else 0)]
            return jnp.concatenate([s[0:ATT_BLOCK] + bias, s[ATT_BLOCK:] + bias], axis=0)

        def values(m, p, mxs, dil=dil, bpr=bpr, wa_s=wa_s, wb_s=wb_s):
            row0 = m * ATT_BLOCK
            p2 = jnp.concatenate([p[0:ATT_BLOCK, :], p[ATT_BLOCK:, :]], axis=1)
            w = jnp.concatenate([key_rows(wa_s, m, dil), key_rows(wb_s, m, dil)], axis=0)
            res = jnp.dot(p2, w, preferred_element_type=F32)
            for c in range(4):
                lo = c * sub
                num, den = res[lo:lo + sub, 0:LANES], res[lo:lo + sub, LANES:]
                mx = jnp.where(first_head, mxs[lo:lo + sub, :], mxs[ATT_BLOCK + lo:ATT_BLOCK + lo + sub, :])
                if dil == 16:
                    r16, n = m // bpr, m % bpr
                    dst = pl.ds((r16 % 4) * S4 + 4 * (ATT_BLOCK * n + lo) + r16 // 4, sub, stride=4)
                elif dil == 4:
                    dst = pl.ds(row0 + lo, sub)
                else:
                    dst = pl.ds(c * S4 + sub * m, sub)
                if dil != 16:
                    mx, num, den = merge(mx_p[dst, :], num_p[dst, :], den_p[dst, :], mx, num, den)
                if dil == 1:
                    o_ref[pl.ds(row0 + c, sub, stride=4), :] = num / den
                else:
                    mx_p[dst, :] = mx
                    num_p[dst, :] = num
                    den_p[dst, :] = den

        s = scores(0)
        p, mxs = softmax(s)
        s = scores(1)
        for m in range(1, nblk - 1):
            values(m - 1, p, mxs)
            p, mxs = softmax(s)
            s = scores(m + 1)
        values(nblk - 2, p, mxs)
        p, mxs = softmax(s)
        values(nblk - 1, p, mxs)


def _attn(q, k, v):
    B, _, S4, _ = q.shape
    S = 4 * S4
    blk = pl.BlockSpec((None, S, LANES), lambda b, p: (b, 0, p))
    mod4 = pl.BlockSpec((None, 4, S4, LANES), lambda b, p: (b, 0, 0, p))
    pad = S + ATT_BLOCK
    plane = pltpu.VMEM((S, LANES), F32)
    operands = [pltpu.VMEM((S, LANES), BF16), pltpu.VMEM((S, LANES), BF16),
                pltpu.VMEM((pad, LANES), BF16),
                pltpu.VMEM((pad, 2 * LANES), BF16), pltpu.VMEM((pad, 2 * LANES), BF16)]
    return pl.pallas_call(
        functools.partial(_attn_kernel, S=S),
        grid=(B, HEAD_PAIRS),
        in_specs=[mod4, mod4, mod4],
        out_specs=blk,
        out_shape=jax.ShapeDtypeStruct((B, S, ATT_WIDTH), F32),
        scratch_shapes=[
            *operands, *operands,
            pltpu.VMEM((4, ATT_BLOCK, 2 * ATT_BLOCK), F32),
            plane, plane, plane,
        ],
        compiler_params=pltpu.CompilerParams(dimension_semantics=("parallel", "parallel"),
                                             vmem_limit_bytes=VMEM_LIMIT),
        name="attn",
    )(q, k, v)


def _rope_tables(S):
    inv = ROPE_THETA ** (-jnp.arange(0, ATT_HEAD_DIM, 2, dtype=F32) / ATT_HEAD_DIM)
    ang = jnp.arange(S, dtype=F32)[:, None] * inv[None, :]
    ang = jnp.concatenate([ang, ang, ang, ang], axis=-1)
    cos, sin = jnp.cos(ang), jnp.sin(ang)
    low_half = (jnp.arange(LANES) % ATT_HEAD_DIM) < ATT_HEAD_DIM // 2
    return cos, jnp.where(low_half, -sin, 0.0), jnp.where(low_half, 0.0, sin)


def kernel(x, c, ada_w, ada_b, norm_g, ffn1_wg, ffn1_wu, ffn1_wd, ffn2_wg, ffn2_wu, ffn2_wd,
           w_in, sgu_ln_g, sgu_ln_b, sgu_w, sgu_b, w_out, final_g):
    B, S, _ = x.shape
    mod = _ada(c, ada_w, ada_b).reshape(DEPTH, N_ADA, B, 1, D_MODEL)
    cos, sina, sinb = _rope_tables(S)
    bs_b = jnp.broadcast_to(sgu_b[..., None], (DEPTH, SGU_HEADS, CHUNK, SGU_HEAD_DIM))
    wg1, wu1, wd1, wg2, wu2, wd2, win, wo = [
        _to_bf16(w) for w in (ffn1_wg, ffn1_wu, ffn1_wd, ffn2_wg, ffn2_wu, ffn2_wd, w_in, w_out)]
    h = x
    for l in range(DEPTH):
        h = _ffn(h, mod[l], norm_g[l, 0], wg1, wu1, wd1, final_g, l=l, sub=0)
        oa, q, k, v = _proj(h, mod[l], norm_g[l, 1], win, sgu_ln_g[l], sgu_ln_b[l], sgu_w[l], bs_b[l],
                            cos, sina, sinb, l=l)
        ob = _attn(q, k, v)
        h = _ffn(h, mod[l], norm_g[l, 2], wg2, wu2, wd2, final_g, l=l, sub=2,
                 mix_args=(oa, ob, wo), final=(l == DEPTH - 1))
    return h
```

```python
import functools
import math

import jax
import jax.numpy as jnp
from jax import lax
from jax.experimental import pallas as pl
from jax.experimental.pallas import tpu as pltpu

D_MODEL = 1024
D_FF = 2816
DEPTH = 2
N_ADA = 9
ADA_COLS = 3
EPS = 1e-6
SGU_WIDTH = 512
SGU_HEADS = 4
SGU_HEAD_DIM = 128
CHUNK = 128
ATT_WIDTH = 512
ATT_HEADS = 8
ATT_HEAD_DIM = 64
ATT_BLOCK = 128
DILATED_PATTERNS = ((128, 1), (512, 4), (2048, 16))
ROPE_THETA = 10000.0
IN_WIDTH = 2 * SGU_WIDTH + 3 * ATT_WIDTH

FFN_ROWS = 256
LANES = 128
HEAD_PAIRS = ATT_WIDTH // LANES
MASK_BIAS = -1e30
VMEM_LIMIT = 56 * 1024 * 1024

BF16 = jnp.bfloat16
F32 = jnp.float32


def _rms_mod(h, ng, sc, sh):
    ms = jnp.mean(h * h, axis=-1, keepdims=True)
    y = h * lax.rsqrt(ms + EPS) * ng
    return y * (1.0 + sc) + sh


def _ada_kernel(c_ref, w_ref, b_ref, o_ref):
    c = c_ref[...]
    c_act = c * jax.nn.sigmoid(c)
    mod = jnp.dot(c_act, w_ref[...], preferred_element_type=F32, precision=lax.Precision.HIGHEST)
    for j in range(ADA_COLS):
        o_ref[j] = mod[:, j * D_MODEL:(j + 1) * D_MODEL] + b_ref[j]


def _ada(c, ada_w, ada_b):
    B = c.shape[0]
    b4 = ada_b.reshape(DEPTH, N_ADA, 1, D_MODEL)
    return pl.pallas_call(
        _ada_kernel,
        grid=(DEPTH, N_ADA // ADA_COLS),
        in_specs=[
            pl.BlockSpec((B, D_MODEL), lambda l, j: (0, 0)),
            pl.BlockSpec((None, D_MODEL, ADA_COLS * D_MODEL), lambda l, j: (l, 0, j)),
            pl.BlockSpec((None, ADA_COLS, 1, D_MODEL), lambda l, j: (l, j, 0, 0)),
        ],
        out_specs=pl.BlockSpec((None, ADA_COLS, B, D_MODEL), lambda l, j: (l, j, 0, 0)),
        out_shape=jax.ShapeDtypeStruct((DEPTH, N_ADA, B, D_MODEL), F32),
        compiler_params=pltpu.CompilerParams(dimension_semantics=("parallel", "parallel"),
                                             vmem_limit_bytes=VMEM_LIMIT),
        name="ada",
    )(c, ada_w, b4)


def _cast_kernel(x_ref, o_ref):
    o_ref[...] = x_ref[...].astype(o_ref.dtype)


def _to_bf16(w):
    depth, R, C = w.shape
    tr = R // 2
    spec = pl.BlockSpec((None, tr, C), lambda l, i: (l, i, 0))
    return pl.pallas_call(
        _cast_kernel,
        grid=(depth, R // tr),
        in_specs=[spec],
        out_specs=spec,
        out_shape=jax.ShapeDtypeStruct(w.shape, BF16),
        compiler_params=pltpu.CompilerParams(dimension_semantics=("parallel", "parallel"),
                                             vmem_limit_bytes=VMEM_LIMIT),
        name="cast",
    )(w)


def _ffn_kernel(*refs, mix, final, fc):
    if mix:
        (h_ref, oa_ref, ob_ref, g2_ref, woa_ref, wob_ref, sh_ref, sc_ref, g_ref, ng_ref,
         wg_ref, wu_ref, wd_ref, fg_ref, o_ref, acc_ref) = refs
    else:
        (h_ref, sh_ref, sc_ref, g_ref, ng_ref, wg_ref, wu_ref, wd_ref, fg_ref, o_ref, acc_ref) = refs
    assert sum(fc) == D_FF and h_ref.shape[0] % FFN_ROWS == 0
    groups = []
    for r in range(h_ref.shape[0] // FFN_ROWS):
        rows = slice(r * FFN_ROWS, (r + 1) * FFN_ROWS)
        h = h_ref[rows, :]
        if mix:
            mixed = jnp.dot(oa_ref[rows, :], woa_ref[...], preferred_element_type=F32)
            ob = jnp.concatenate([ob_ref[p, rows, :] for p in range(HEAD_PAIRS)], axis=1).astype(BF16)
            mixed = mixed + jnp.dot(ob, wob_ref[...], preferred_element_type=F32)
            h = h + g2_ref[...] * mixed
        y = _rms_mod(h, ng_ref[...], sc_ref[...], sh_ref[...]).astype(BF16)
        groups.append((rows, h, y))
    for rows, h, y in groups:
        for i in range(len(fc)):
            lo, hi = sum(fc[:i]), sum(fc[:i + 1])
            gate = jnp.dot(y, wg_ref[:, lo:hi], preferred_element_type=F32)
            up = jnp.dot(y, wu_ref[:, lo:hi], preferred_element_type=F32)
            act = (gate * jax.nn.sigmoid(gate) * up).astype(BF16)
            part = jnp.dot(act, wd_ref[lo:hi, :], preferred_element_type=F32)
            if i == 0:
                acc_ref[rows, :] = part
            else:
                acc_ref[rows, :] += part
        out = h + (0.5 * g_ref[...]) * acc_ref[rows, :]
        if final:
            ms = jnp.mean(out * out, axis=-1, keepdims=True)
            out = out * lax.rsqrt(ms + EPS) * fg_ref[...]
        o_ref[rows, :] = out


def _ffn(h, mod_l, ng, wg, wu, wd, fg, *, l, sub, mix_args=None, final=False, tm=1024, fc=(1536, 1280)):
    B, S, _ = h.shape
    mix = mix_args is not None
    row = lambda j: pl.BlockSpec((None, None, 1, D_MODEL), lambda b, i, j=j: (j, b, 0, 0))
    const = lambda shape: pl.BlockSpec(shape, lambda b, i: (0,) * len(shape),
                                       pipeline_mode=pl.Buffered(1))
    layer = lambda shape, r=0: pl.BlockSpec((None,) + shape, lambda b, i: (l, r, 0),
                                            pipeline_mode=pl.Buffered(1))
    tile = lambda w: pl.BlockSpec((None, tm, w), lambda b, i: (b, i, 0))
    args = [h]
    specs = [tile(D_MODEL)]
    if mix:
        oa, ob, wo = mix_args
        args += [oa, ob, mod_l, wo, wo]
        specs += [tile(SGU_WIDTH), pl.BlockSpec((None, HEAD_PAIRS, tm, LANES), lambda b, i: (b, 0, i, 0)), row(5),
                  layer((SGU_WIDTH, D_MODEL), 0), layer((ATT_WIDTH, D_MODEL), 1)]
    args += [mod_l, mod_l, mod_l, ng.reshape(1, D_MODEL), wg, wu, wd, fg.reshape(1, D_MODEL)]
    specs += [row(3 * sub), row(3 * sub + 1), row(3 * sub + 2), const((1, D_MODEL)),
              layer((D_MODEL, D_FF)), layer((D_MODEL, D_FF)), layer((D_FF, D_MODEL)),
              const((1, D_MODEL))]
    return pl.pallas_call(
        functools.partial(_ffn_kernel, mix=mix, final=final, fc=fc),
        grid=(B, S // tm),
        in_specs=specs,
        out_specs=tile(D_MODEL),
        out_shape=jax.ShapeDtypeStruct((B, S, D_MODEL), F32),
        scratch_shapes=[pltpu.VMEM((tm, D_MODEL), F32)],
        compiler_params=pltpu.CompilerParams(dimension_semantics=("parallel", "parallel"),
                                             vmem_limit_bytes=VMEM_LIMIT),
        name="ffn_mix" if mix else "ffn",
    )(*args)


def _proj_kernel(h_ref, sh_ref, sc_ref, ng_ref, win_ref, lng_ref, lnb_ref, ws_ref, bs_ref,
                 cos_ref, sina_ref, sinb_ref, oa_ref, q_ref, k_ref, v_ref, nat_s, *, tm):
    y = _rms_mod(h_ref[...], ng_ref[...], sc_ref[...], sh_ref[...]).astype(BF16)
    proj = jnp.dot(y, win_ref[...], preferred_element_type=F32)

    row = lax.broadcasted_iota(jnp.int32, (CHUNK, CHUNK), 0)
    col = lax.broadcasted_iota(jnp.int32, (CHUNK, CHUNK), 1)
    for hd in range(SGU_HEADS):
        lo = hd * SGU_HEAD_DIM
        u = jax.nn.gelu(proj[:, lo:lo + SGU_HEAD_DIM])
        v = jax.nn.gelu(proj[:, SGU_WIDTH + lo:SGU_WIDTH + lo + SGU_HEAD_DIM])
        mu = jnp.mean(v, axis=-1, keepdims=True)
        var = jnp.mean(jnp.square(v - mu), axis=-1, keepdims=True)
        vn = ((v - mu) * lax.rsqrt(var + EPS)) * lng_ref[hd:hd + 1, :] + lnb_ref[hd:hd + 1, :]
        vn = vn.astype(BF16)
        ws = jnp.where(row >= col, ws_ref[hd], 0.0).astype(BF16)
        for c in range(tm // CHUNK):
            r0 = c * CHUNK
            z = jnp.dot(ws, vn[r0:r0 + CHUNK, :], preferred_element_type=F32) + bs_ref[hd]
            oa_ref[r0:r0 + CHUNK, lo:lo + SGU_HEAD_DIM] = (u[r0:r0 + CHUNK, :] * z).astype(oa_ref.dtype)

    cos = cos_ref[...]
    sina = sina_ref[...]
    sinb = sinb_ref[...]
    quarter = ATT_HEAD_DIM // 2
    scale = math.log2(math.e) / math.sqrt(ATT_HEAD_DIM)
    for p in range(HEAD_PAIRS):
        lo = 2 * SGU_WIDTH + p * LANES
        for t, mul in ((0, scale), (1, 1.0)):
            x = proj[:, lo + t * ATT_WIDTH:lo + t * ATT_WIDTH + LANES]
            ahead = pltpu.roll(x, LANES - quarter, 1)
            behind = pltpu.roll(x, quarter, 1)
            r = x * cos + ahead * sina + behind * sinb
            nat_s[t * HEAD_PAIRS + p] = r * mul if mul != 1.0 else r
        nat_s[2 * HEAD_PAIRS + p] = proj[:, lo + 2 * ATT_WIDTH:lo + 2 * ATT_WIDTH + LANES]
    for t, ref in enumerate((q_ref, k_ref, v_ref)):
        for p in range(HEAD_PAIRS):
            for r4 in range(4):
                ref[p, r4] = nat_s[t * HEAD_PAIRS + p, pl.ds(r4, tm // 4, stride=4), :]


def _proj(h, mod_l, ng, win, lng, lnb, ws, bs_b, cos, sina, sinb, *, l, tm=1024):
    B, S, _ = h.shape
    row = lambda j: pl.BlockSpec((None, None, 1, D_MODEL), lambda b, i, j=j: (j, b, 0, 0))
    const = lambda shape: pl.BlockSpec(shape, lambda b, i: (0,) * len(shape),
                                       pipeline_mode=pl.Buffered(1))
    tile = lambda w: pl.BlockSpec((None, tm, w), lambda b, i: (b, i, 0))
    tab = pl.BlockSpec((tm, LANES), lambda b, i: (i, 0))
    mod4 = pl.BlockSpec((None, HEAD_PAIRS, 4, tm // 4, LANES), lambda b, i: (b, 0, 0, i, 0))
    return pl.pallas_call(
        functools.partial(_proj_kernel, tm=tm),
        grid=(B, S // tm),
        in_specs=[tile(D_MODEL), row(3), row(4), const((1, D_MODEL)),
                  pl.BlockSpec((None, D_MODEL, IN_WIDTH), lambda b, i: (l, 0, 0), pipeline_mode=pl.Buffered(1)),
                  const((SGU_HEADS, SGU_HEAD_DIM)), const((SGU_HEADS, SGU_HEAD_DIM)),
                  const((SGU_HEADS, CHUNK, CHUNK)), const((SGU_HEADS, CHUNK, SGU_HEAD_DIM)),
                  tab, tab, tab],
        out_specs=[tile(SGU_WIDTH), mod4, mod4, mod4],
        out_shape=[jax.ShapeDtypeStruct((B, S, SGU_WIDTH), BF16)]
        + [jax.ShapeDtypeStruct((B, HEAD_PAIRS, 4, S // 4, LANES), F32)] * 3,
        scratch_shapes=[pltpu.VMEM((3 * HEAD_PAIRS, tm, LANES), F32)],
        compiler_params=pltpu.CompilerParams(dimension_semantics=("parallel", "parallel"),
                                             vmem_limit_bytes=VMEM_LIMIT),
        name="proj",
    )(h, mod_l, mod_l, ng.reshape(1, D_MODEL), win, lng, lnb, ws, bs_b, cos, sina, sinb)


def _attn_kernel(q_ref, k_ref, v_ref, o_ref, qa16, qb16, k16, wa16, wb16, qa4, qb4, k4, wa4, wb4, bias_s,
                 num_p, den_p, mx_p, *, S):
    nblk = S // ATT_BLOCK
    S4 = S // 4
    sub = ATT_BLOCK // 4
    assert S % (16 * ATT_BLOCK) == 0 and nblk >= 3
    lane = lax.broadcasted_iota(jnp.int32, (1, LANES), 1)
    first_head = lane < ATT_HEAD_DIM
    nt = (((1,), (1,)), ((), ()))

    qi = lax.broadcasted_iota(jnp.int32, (ATT_BLOCK, 2 * ATT_BLOCK), 0)
    ki = lax.broadcasted_iota(jnp.int32, (ATT_BLOCK, 2 * ATT_BLOCK), 1)
    for tab, qpos, kpos in ((0, qi, ki), (2, 4 * (qi % sub) + qi // sub, 4 * (ki % (2 * sub)) + ki // (2 * sub))):
        diff = qpos + ATT_BLOCK - kpos
        band = (diff >= 0) & (diff <= ATT_BLOCK)
        bias_s[tab] = jnp.where(band, 0.0, MASK_BIAS)
        bias_s[tab + 1] = jnp.where(band & (kpos >= ATT_BLOCK), 0.0, MASK_BIAS)

    rows = S + ATT_BLOCK
    zpad = jnp.zeros((ATT_BLOCK, LANES), BF16)
    for k_s, wa_s, wb_s in ((k16, wa16, wb16), (k4, wa4, wb4)):
        wa_s[:, LANES:] = jnp.broadcast_to(jnp.where(first_head, 1.0, 0.0).astype(BF16), (rows, LANES))
        wb_s[:, LANES:] = jnp.broadcast_to(jnp.where(first_head, 0.0, 1.0).astype(BF16), (rows, LANES))
        k_s[0:ATT_BLOCK, :] = zpad
        wa_s[0:ATT_BLOCK, 0:LANES] = zpad
        wb_s[0:ATT_BLOCK, 0:LANES] = zpad

    cp = 256
    nchunk = S // cp

    def put_operands(ops, dst, q, k, v):
        qa_s, qb_s, k_s, wa_s, wb_s = ops
        qa_s[pl.ds(dst, cp), :] = jnp.where(first_head, q, 0.0).astype(BF16)
        qb_s[pl.ds(dst, cp), :] = jnp.where(first_head, 0.0, q).astype(BF16)
        k_s[pl.ds(dst + ATT_BLOCK, cp), :] = k.astype(BF16)
        wa_s[pl.ds(dst + ATT_BLOCK, cp), 0:LANES] = jnp.where(first_head, v, 0.0).astype(BF16)
        wb_s[pl.ds(dst + ATT_BLOCK, cp), 0:LANES] = jnp.where(first_head, 0.0, v).astype(BF16)

    def key_rows(ref, m, dil):
        if dil == 1:
            return jnp.concatenate(
                [ref[pl.ds(ATT_BLOCK + r4 * S4 + sub * (m - 1), 2 * sub), :] for r4 in range(4)], axis=0)
        return ref[pl.ds(m * ATT_BLOCK, 2 * ATT_BLOCK), :]

    def merge(m0, n0, d0, m1, n1, d1):
        mm = jnp.maximum(m0, m1)
        a0, a1 = jnp.exp2(m0 - mm), jnp.exp2(m1 - mm)
        return mm, a0 * n0 + a1 * n1, a0 * d0 + a1 * d1

    def softmax(s):
        mx = jnp.max(s, axis=-1, keepdims=True)
        return jnp.exp2(s - mx).astype(BF16), mx

    for window, dil in reversed(DILATED_PATTERNS):
        assert window // dil == ATT_BLOCK and dil in (1, 4, 16)
        bpr = S // dil // ATT_BLOCK
        ops = (qa16, qb16, k16, wa16, wb16) if dil == 16 else (qa4, qb4, k4, wa4, wb4)
        qa_s, qb_s, k_s, wa_s, wb_s = ops

        if dil == 16:
            for i in range(nchunk):
                src = pl.ds(i // 4, cp, stride=4)
                put_operands(ops, i * cp, q_ref[i % 4, src, :], k_ref[i % 4, src, :], v_ref[i % 4, src, :])
        elif dil == 4:
            for i in range(nchunk):
                src = pl.ds((i % (S4 // cp)) * cp, cp)
                r4 = i // (S4 // cp)
                put_operands(ops, i * cp, q_ref[r4, src, :], k_ref[r4, src, :], v_ref[r4, src, :])

        def scores(m, dil=dil, bpr=bpr, qa_s=qa_s, qb_s=qb_s, k_s=k_s):
            row0 = m * ATT_BLOCK
            if dil == 1:
                picks = [ref[pl.ds(r4 * S4 + sub * m, sub), :] for ref in (qa_s, qb_s) for r4 in range(4)]
            else:
                picks = [qa_s[pl.ds(row0, ATT_BLOCK), :], qb_s[pl.ds(row0, ATT_BLOCK), :]]
            q2 = jnp.concatenate(picks, axis=0)
            s = lax.dot_general(q2, key_rows(k_s, m, dil), nt, preferred_element_type=F32)
            bias = bias_s[(2 if dil == 1 else 0) + (1 if m % bpr == 0 else 0)]
            return jnp.concatenate([s[0:ATT_BLOCK] + bias, s[ATT_BLOCK:] + bias], axis=0)

        def values(m, p, mxs, dil=dil, bpr=bpr, wa_s=wa_s, wb_s=wb_s):
            row0 = m * ATT_BLOCK
            p2 = jnp.concatenate([p[0:ATT_BLOCK, :], p[ATT_BLOCK:, :]], axis=1)
            w = jnp.concatenate([key_rows(wa_s, m, dil), key_rows(wb_s, m, dil)], axis=0)
            res = jnp.dot(p2, w, preferred_element_type=F32)
            for c in range(4):
                lo = c * sub
                num, den = res[lo:lo + sub, 0:LANES], res[lo:lo + sub, LANES:]
                mx = jnp.where(first_head, mxs[lo:lo + sub, :], mxs[ATT_BLOCK + lo:ATT_BLOCK + lo + sub, :])
                if dil == 16:
                    r16, n = m // bpr, m % bpr
                    dst = pl.ds((r16 % 4) * S4 + 4 * (ATT_BLOCK * n + lo) + r16 // 4, sub, stride=4)
                elif dil == 4:
                    dst = pl.ds(row0 + lo, sub)
                else:
                    dst = pl.ds(c * S4 + sub * m, sub)
                if dil != 16:
                    mx, num, den = merge(mx_p[dst, :], num_p[dst, :], den_p[dst, :], mx, num, den)
                if dil == 1:
                    o_ref[pl.ds(row0 + c, sub, stride=4), :] = num / den
                else:
                    mx_p[dst, :] = mx
                    num_p[dst, :] = num
                    den_p[dst, :] = den

        s = scores(0)
        p, mxs = softmax(s)
        s = scores(1)
        for m in range(1, nblk - 1):
            values(m - 1, p, mxs)
            p, mxs = softmax(s)
            s = scores(m + 1)
        values(nblk - 2, p, mxs)
        p, mxs = softmax(s)
        values(nblk - 1, p, mxs)


def _attn(q, k, v):
    B, _, _, S4, _ = q.shape
    S = 4 * S4
    blk = pl.BlockSpec((None, None, S, LANES), lambda b, p: (b, p, 0, 0))
    mod4 = pl.BlockSpec((None, None, 4, S4, LANES), lambda b, p: (b, p, 0, 0, 0))
    pad = S + ATT_BLOCK
    plane = pltpu.VMEM((S, LANES), F32)
    operands = [pltpu.VMEM((S, LANES), BF16), pltpu.VMEM((S, LANES), BF16),
                pltpu.VMEM((pad, LANES), BF16),
                pltpu.VMEM((pad, 2 * LANES), BF16), pltpu.VMEM((pad, 2 * LANES), BF16)]
    return pl.pallas_call(
        functools.partial(_attn_kernel, S=S),
        grid=(B, HEAD_PAIRS),
        in_specs=[mod4, mod4, mod4],
        out_specs=blk,
        out_shape=jax.ShapeDtypeStruct((B, HEAD_PAIRS, S, LANES), F32),
        scratch_shapes=[
            *operands, *operands,
            pltpu.VMEM((4, ATT_BLOCK, 2 * ATT_BLOCK), F32),
            plane, plane, plane,
        ],
        compiler_params=pltpu.CompilerParams(dimension_semantics=("parallel", "parallel"),
                                             vmem_limit_bytes=VMEM_LIMIT),
        name="attn",
    )(q, k, v)


def _rope_tables(S):
    inv = ROPE_THETA ** (-jnp.arange(0, ATT_HEAD_DIM, 2, dtype=F32) / ATT_HEAD_DIM)
    ang = jnp.arange(S, dtype=F32)[:, None] * inv[None, :]
    ang = jnp.concatenate([ang, ang, ang, ang], axis=-1)
    cos, sin = jnp.cos(ang), jnp.sin(ang)
    low_half = (jnp.arange(LANES) % ATT_HEAD_DIM) < ATT_HEAD_DIM // 2
    return cos, jnp.where(low_half, -sin, 0.0), jnp.where(low_half, 0.0, sin)


def kernel(x, c, ada_w, ada_b, norm_g, ffn1_wg, ffn1_wu, ffn1_wd, ffn2_wg, ffn2_wu, ffn2_wd,
           w_in, sgu_ln_g, sgu_ln_b, sgu_w, sgu_b, w_out, final_g):
    B, S, _ = x.shape
    mod = _ada(c, ada_w, ada_b).reshape(DEPTH, N_ADA, B, 1, D_MODEL)
    cos, sina, sinb = _rope_tables(S)
    bs_b = jnp.broadcast_to(sgu_b[..., None], (DEPTH, SGU_HEADS, CHUNK, SGU_HEAD_DIM))
    wg1, wu1, wd1, wg2, wu2, wd2, win, wo = [
        _to_bf16(w) for w in (ffn1_wg, ffn1_wu, ffn1_wd, ffn2_wg, ffn2_wu, ffn2_wd, w_in, w_out)]
    h = x
    for l in range(DEPTH):
        h = _ffn(h, mod[l], norm_g[l, 0], wg1, wu1, wd1, final_g, l=l, sub=0)
        oa, q, k, v = _proj(h, mod[l], norm_g[l, 1], win, sgu_ln_g[l], sgu_ln_b[l], sgu_w[l], bs_b[l],
                            cos, sina, sinb, l=l)
        ob = _attn(q, k, v)
        h = _ffn(h, mod[l], norm_g[l, 2], wg2, wu2, wd2, final_g, l=l, sub=2,
                 mix_args=(oa, ob, wo), final=(l == DEPTH - 1))
    return h
```
